```python
import math
import jax
import jax.numpy as jnp
from jax import lax
import numpy as np

D_MODEL = 2048
BATCH = 4
SEQ = 2048
DEPTH = 2
DEC_BATCH = 128
DEC_SEQ = 8
PAST_LEN = 8192
PAGE_SIZE = 128

N_BRANCH = 4
MIX_WIDTH = D_MODEL // 4
HEAD_DIM = 64
ROT_DIM = HEAD_DIM // 4
ROPE_THETA = 500000.0
EPS = 1e-6
Q_BLOCK = 128
DEC_CHUNK = 4
SEL_FORCE = 1e30

MLA_HEADS = 8
MLA_NOPE = 64
MLA_ROPE = 32
MLA_QK = MLA_NOPE + MLA_ROPE
MLA_V = MIX_WIDTH // MLA_HEADS
MLA_Q_LORA = 384
MLA_KV_LORA = 128
MLA_THETA = 10000.0

MOBA_HEADS = MIX_WIDTH // HEAD_DIM
MOBA_KV_HEADS = 2
MOBA_BLOCK = 256
MOBA_TOPK = 3
MOBA_Q_BLOCK = 16

DIFF_HEADS = 4
DIFF_KV_HEADS = 2
DIFF_DIM = 64
DIFF_VDIM = 2 * DIFF_DIM

NSA_HEADS = MIX_WIDTH // HEAD_DIM
NSA_KV_HEADS = 1
NSA_CMP_LEN = 32
NSA_CMP_STRIDE = 16
NSA_CMP_HIDDEN = 128
NSA_SEL_BLOCK = 64
NSA_TOPN = 16
NSA_WINDOW = 512

FFN_HIDDEN = -(-8 * D_MODEL // (3 * 256)) * 256

IN_WIDTHS = (
    MLA_Q_LORA, MLA_KV_LORA, MLA_ROPE,
    MOBA_HEADS * HEAD_DIM, 2 * MOBA_KV_HEADS * HEAD_DIM,
    DIFF_HEADS * 2 * DIFF_DIM, 2 * DIFF_KV_HEADS * DIFF_VDIM,
    NSA_HEADS * HEAD_DIM, 6 * NSA_KV_HEADS * HEAD_DIM, 3 * NSA_HEADS,
    N_BRANCH * D_MODEL,
)
IN_DIM = sum(IN_WIDTHS)

kernel_name = 'hybrid_mla_moba_diff_nsa_decode_step'


def rmsnorm(x, g):
    xf = x.astype(jnp.float32)
    y = xf * lax.rsqrt(jnp.mean(xf * xf, axis=-1, keepdims=True) + EPS)
    return (y * g.astype(jnp.float32)).astype(x.dtype)


def rope(x, pos, rot_dim, theta):
    half = rot_dim // 2
    inv = theta ** (-jnp.arange(half, dtype=jnp.float32) / half)
    ang = pos.astype(jnp.float32)[:, None] * inv[None, :]
    shp = (pos.shape[0],) + (1,) * (x.ndim - 3) + (half,)
    cos = jnp.cos(ang).reshape(shp)
    sin = jnp.sin(ang).reshape(shp)
    xr = x[..., :rot_dim].astype(jnp.float32)
    x1, x2 = xr[..., :half], xr[..., half:]
    rot = jnp.concatenate([x1 * cos - x2 * sin, x2 * cos + x1 * sin], axis=-1).astype(x.dtype)
    return jnp.concatenate([rot, x[..., rot_dim:]], axis=-1)


def masked_softmax(s, mask):
    s = jnp.where(mask, s, -jnp.inf)
    m = jnp.max(s, axis=-1, keepdims=True)
    m = jnp.where(jnp.isfinite(m), m, 0.0)
    e = jnp.where(mask, jnp.exp(s - m), 0.0)
    return e / jnp.maximum(jnp.sum(e, axis=-1, keepdims=True), 1e-30)


def causal_mask(qpos, kpos):
    return kpos[None, :] <= qpos[:, None]


def window_mask(qpos, kpos):
    d = qpos[:, None] - kpos[None, :]
    return (d >= 0) & (d < NSA_WINDOW) & (kpos[None, :] >= 0)


def attend(q, k, v, mask):
    s = jnp.einsum('bqgrd,bkgd->bgrqk', q, k).astype(jnp.float32) * (q.shape[-1] ** -0.5)
    p = masked_softmax(s, mask)
    return jnp.einsum('bgrqk,bkgd->bqgrd', p.astype(v.dtype), v)


def lambda_init(li):
    return 0.8 - 0.6 * math.exp(-0.3 * li)


def diff_lambda(lam_vecs, li):
    lv = lam_vecs.astype(jnp.float32)
    return jnp.exp(jnp.sum(lv[0] * lv[1])) - jnp.exp(jnp.sum(lv[2] * lv[3])) + lambda_init(li)


def mla_expand(ckv, kpe, kpos, lw):
    k_nope = jnp.einsum('btr,rhd->bthd', ckv, lw['mla_w_uk'])
    k_pe = jnp.broadcast_to(kpe[:, :, None, :], k_nope.shape[:3] + (MLA_ROPE,))
    k = jnp.concatenate([k_pe, k_nope], axis=-1)
    k = rope(rmsnorm(k, lw['mla_kn_g']), kpos, MLA_ROPE, MLA_THETA)
    v = jnp.einsum('btr,rhd->bthd', ckv, lw['mla_w_uv'])
    return k, v


def moba_attn(q, k, v, qpos):
    b, tq, g, r, d = q.shape
    tk = k.shape[1]
    nb = -(-tk // MOBA_BLOCK)
    pad = nb * MOBA_BLOCK - tk

    def blocks(a):
        a = jnp.pad(a, ((0, 0), (0, pad), (0, 0), (0, 0)))
        return a.reshape(b, nb, MOBA_BLOCK, g, d).transpose(0, 3, 1, 2, 4)

    kb, vb = blocks(k), blocks(v)
    k_mean = jnp.mean(kb.astype(jnp.float32), axis=3)
    gate = jnp.einsum('bqgrd,bgnd->bqgrn', q.astype(jnp.float32), k_mean)
    cur = qpos // MOBA_BLOCK
    past = jnp.arange(nb)[None, :] < cur[:, None]
    gate = jnp.where(past[None, :, None, None, :], gate, -jnp.inf)
    top_s, top_i = lax.top_k(gate, min(MOBA_TOPK, nb))
    idx = jnp.concatenate([top_i, jnp.broadcast_to(cur[None, :, None, None, None], top_i.shape[:-1] + (1,))], axis=-1)
    ok = jnp.concatenate([jnp.isfinite(top_s), jnp.ones(top_s.shape[:-1] + (1,), bool)], axis=-1)
    bi = jnp.arange(b)[:, None, None, None, None]
    gi = jnp.arange(g)[None, None, :, None, None]
    kg = kb[bi, gi, idx]
    vg = vb[bi, gi, idx]
    kpos = idx[..., None] * MOBA_BLOCK + jnp.arange(MOBA_BLOCK)
    mask = ok[..., None] & (kpos <= qpos[None, :, None, None, None, None])
    nk = idx.shape[-1] * MOBA_BLOCK
    s = jnp.einsum('bqgrd,bqgrkld->bqgrkl', q, kg).astype(jnp.float32) * (d ** -0.5)
    p = masked_softmax(s.reshape(b, tq, g, r, nk), mask.reshape(b, tq, g, r, nk))
    return jnp.einsum('bqgrm,bqgrmd->bqgrd', p.astype(v.dtype), vg.reshape(b, tq, g, r, nk, d))


def diff_attn(q, k, v, qpos, kpos, lam):
    s = jnp.einsum('bqgrcd,bkgcd->bgrcqk', q, k).astype(jnp.float32) * (q.shape[-1] ** -0.5)
    p = masked_softmax(s, causal_mask(qpos, kpos))
    a = p[:, :, :, 0] - lam * p[:, :, :, 1]
    return jnp.einsum('bgrqk,bkgd->bqgrd', a.astype(v.dtype), v)


def nsa_compress(k_rows, v_rows, lw):
    b, tk, g, d = k_rows.shape
    nc = (tk - NSA_CMP_LEN) // NSA_CMP_STRIDE + 1
    idx = jnp.arange(nc)[:, None] * NSA_CMP_STRIDE + jnp.arange(NSA_CMP_LEN)[None, :]

    def squeeze_blocks(rows, j):
        blk = rows[:, idx] + lw['nsa_cmp_pos'][j][None, None, :, None, :]
        blk = jnp.swapaxes(blk, 2, 3).reshape(b, nc, g, NSA_CMP_LEN * d)
        hid = jax.nn.gelu(blk @ lw['nsa_cmp_w1'][j] + lw['nsa_cmp_b1'][j])
        return hid @ lw['nsa_cmp_w2'][j]

    kc = squeeze_blocks(k_rows, 0)
    vc = squeeze_blocks(v_rows, 1)
    cend = idx[:, -1]
    kc = rope(rmsnorm(kc, lw['nsa_kn_g'][0]), cend, ROT_DIM, ROPE_THETA)
    return kc, vc, cend


def nsa_attn(q, gates, kc, vc, cend, ks, vs, qpos):
    b, tq, g, r, d = q.shape
    tk = ks.shape[1]
    s = jnp.einsum('bqgrd,bngd->bgrqn', q, kc).astype(jnp.float32) * (d ** -0.5)
    p_cmp = masked_softmax(s, cend[None, :] <= qpos[:, None])
    o_cmp = jnp.einsum('bgrqn,bngd->bqgrd', p_cmp.astype(vc.dtype), vc)
    nsb = -(-tk // NSA_SEL_BLOCK)
    c_start = jnp.arange(kc.shape[1]) * NSA_CMP_STRIDE
    b_start = jnp.arange(nsb) * NSA_SEL_BLOCK
    covers = ((c_start[:, None] < b_start[None, :] + NSA_SEL_BLOCK)
              & (c_start[:, None] + NSA_CMP_LEN > b_start[None, :])).astype(jnp.float32)
    imp = jnp.einsum('bgrqn,ns->bqgs', p_cmp, covers)
    cur = qpos // NSA_SEL_BLOCK
    blk = jnp.arange(nsb)
    forced = (blk[None, :] == cur[:, None]) | (blk[None, :] == 0)
    avail = blk[None, :] <= cur[:, None]
    score = jnp.where(forced[None, :, None, :], SEL_FORCE,
                      jnp.where(avail[None, :, None, :], imp, -jnp.inf))
    n = min(NSA_TOPN, nsb)
    top_s, top_i = lax.top_k(score, n)
    ok = jnp.isfinite(top_s)
    pad = nsb * NSA_SEL_BLOCK - tk

    def blocks(a):
        a = jnp.pad(a, ((0, 0), (0, pad), (0, 0), (0, 0)))
        return a.reshape(b, nsb, NSA_SEL_BLOCK, g, d).transpose(0, 3, 1, 2, 4)

    bi = jnp.arange(b)[:, None, None, None]
    gi = jnp.arange(g)[None, None, :, None]
    kg = blocks(ks)[bi, gi, top_i]
    vg = blocks(vs)[bi, gi, top_i]
    kpos = top_i[..., None] * NSA_SEL_BLOCK + jnp.arange(NSA_SEL_BLOCK)
    mask = ok[..., None] & (kpos <= qpos[None, :, None, None, None])
    m = n * NSA_SEL_BLOCK
    s2 = jnp.einsum('bqgrd,bqgnld->bqgrnl', q, kg).astype(jnp.float32) * (d ** -0.5)
    p_sel = masked_softmax(s2.reshape(b, tq, g, r, m), mask.reshape(b, tq, g, 1, m))
    o_sel = jnp.einsum('bqgrm,bqgmd->bqgrd', p_sel.astype(vs.dtype), vg.reshape(b, tq, g, m, d))
    return gates[..., 0:1] * o_cmp + gates[..., 1:2] * o_sel


def token_rows(x, pos, lw):
    b, t, _ = x.shape
    h = rmsnorm(x, lw['ln_attn_g'])
    z = h @ lw['w_in']
    cuts = np.cumsum(IN_WIDTHS)[:-1].tolist()
    cq, ckv, kpe, mq, mkv, dq, dkv, nq, nkv, ng, mg = jnp.split(z, cuts, axis=-1)
    q_mla = jnp.einsum('btr,rhd->bthd', rmsnorm(cq, lw['mla_cq_g']), lw['mla_w_uq'])
    q_mla = rope(rmsnorm(q_mla, lw['mla_qn_g']), pos, MLA_ROPE, MLA_THETA)
    ckv = rmsnorm(ckv, lw['mla_ckv_g'])
    q_moba = rope(rmsnorm(mq.reshape(b, t, MOBA_HEADS, HEAD_DIM), lw['moba_qn_g']), pos, ROT_DIM, ROPE_THETA)
    mkv = mkv.reshape(b, t, 2, MOBA_KV_HEADS, HEAD_DIM)
    k_moba = rope(rmsnorm(mkv[:, :, 0], lw['moba_kn_g']), pos, ROT_DIM, ROPE_THETA)
    q_diff = rope(rmsnorm(dq.reshape(b, t, DIFF_HEADS, 2, DIFF_DIM), lw['diff_qn_g']), pos, ROT_DIM, ROPE_THETA)
    dkv = dkv.reshape(b, t, 2, DIFF_KV_HEADS, DIFF_VDIM)
    k_diff = rope(rmsnorm(dkv[:, :, 0].reshape(b, t, DIFF_KV_HEADS, 2, DIFF_DIM), lw['diff_kn_g']), pos, ROT_DIM, ROPE_THETA)
    q_nsa = rope(rmsnorm(nq.reshape(b, t, NSA_HEADS, HEAD_DIM), lw['nsa_qn_g']), pos, ROT_DIM, ROPE_THETA)
    nkv = nkv.reshape(b, t, 6, NSA_KV_HEADS, HEAD_DIM)
    k_sel = rope(rmsnorm(nkv[:, :, 2], lw['nsa_kn_g'][1]), pos, ROT_DIM, ROPE_THETA)
    k_win = rope(rmsnorm(nkv[:, :, 4], lw['nsa_kn_g'][2]), pos, ROT_DIM, ROPE_THETA)
    return {
        'q_mla': q_mla,
        'ckv': ckv,
        'kpe': kpe,
        'q_moba': q_moba.reshape(b, t, MOBA_KV_HEADS, -1, HEAD_DIM),
        'moba_kv': jnp.stack([k_moba, mkv[:, :, 1]], axis=2),
        'q_diff': q_diff.reshape(b, t, DIFF_KV_HEADS, -1, 2, DIFF_DIM),
        'diff_kv': jnp.stack([k_diff.reshape(b, t, DIFF_KV_HEADS, DIFF_VDIM), dkv[:, :, 1]], axis=2),
        'q_nsa': q_nsa.reshape(b, t, NSA_KV_HEADS, -1, HEAD_DIM),
        'g_nsa': jax.nn.sigmoid(ng).reshape(b, t, NSA_KV_HEADS, -1, 3),
        'nsa_kv': jnp.stack([nkv[:, :, 0], nkv[:, :, 1], k_sel, nkv[:, :, 3]], axis=2),
        'nsa_win': jnp.stack([k_win, nkv[:, :, 5]], axis=2),
        'g_merge': jax.nn.sigmoid(mg).reshape(b, t, N_BRANCH, D_MODEL),
    }


def sweep_queries(fn, q_args, block):
    b, t = q_args[0].shape[:2]
    nb = t // block
    xs = tuple(jnp.moveaxis(a.reshape((b, nb, block) + a.shape[2:]), 1, 0) for a in q_args)
    starts = jnp.arange(nb, dtype=jnp.int32) * block
    out = lax.map(lambda a: fn(*a[0], a[1]), (xs, starts))
    out = jnp.moveaxis(out, 0, 1)
    return out.reshape((b, t) + out.shape[3:])


def sweep_batch(fn, args, max_chunk):
    b = args[0].shape[0]
    c = max(d for d in range(1, max_chunk + 1) if b % d == 0)
    xs = tuple(a.reshape((b // c, c) + a.shape[1:]) for a in args)
    outs = lax.map(lambda a: fn(*a), xs)
    return tuple(o.reshape((b,) + o.shape[2:]) for o in outs)


def prompt_attention(r, pos, lw, li):
    b, t = r['ckv'].shape[:2]
    qb = min(Q_BLOCK, t)
    k_mla, v_mla = mla_expand(r['ckv'], r['kpe'], pos, lw)

    def mla_block(q, s0):
        qpos = s0 + jnp.arange(q.shape[1])
        return attend(q[:, :, :, None], k_mla, v_mla, causal_mask(qpos, pos))[:, :, :, 0]

    o_mla = sweep_queries(mla_block, (r['q_mla'],), qb)
    k_moba, v_moba = r['moba_kv'][:, :, 0], r['moba_kv'][:, :, 1]

    def moba_block(q, s0):
        return moba_attn(q, k_moba, v_moba, s0 + jnp.arange(q.shape[1]))

    o_moba = sweep_queries(moba_block, (r['q_moba'],), min(MOBA_Q_BLOCK, t))
    k_diff = r['diff_kv'][:, :, 0].reshape(b, t, DIFF_KV_HEADS, 2, DIFF_DIM)
    v_diff = r['diff_kv'][:, :, 1]
    lam = diff_lambda(lw['diff_lambda'], li)

    def diff_block(q, s0):
        return diff_attn(q, k_diff, v_diff, s0 + jnp.arange(q.shape[1]), pos, lam)

    o_diff = sweep_queries(diff_block, (r['q_diff'],), qb)
    nkv = r['nsa_kv']
    kc, vc, cend = nsa_compress(nkv[:, :, 0], nkv[:, :, 1], lw)

    def nsa_block(q, g, s0):
        return nsa_attn(q, g, kc, vc, cend, nkv[:, :, 2], nkv[:, :, 3], s0 + jnp.arange(q.shape[1]))

    o_cs = sweep_queries(nsa_block, (r['q_nsa'], r['g_nsa']), qb)
    w_pad = jnp.pad(r['nsa_win'], ((0, 0), (NSA_WINDOW, 0), (0, 0), (0, 0), (0, 0)))

    def win_block(q, s0):
        n = q.shape[1]
        band = lax.dynamic_slice_in_dim(w_pad, s0, NSA_WINDOW + n, axis=1)
        kpos = s0 - NSA_WINDOW + jnp.arange(NSA_WINDOW + n)
        return attend(q, band[:, :, 0], band[:, :, 1], window_mask(s0 + jnp.arange(n), kpos))

    o_win = sweep_queries(win_block, (r['q_nsa'],), qb)
    o_nsa = o_cs + r['g_nsa'][..., 2:3] * o_win
    return o_mla, o_moba, o_diff, o_nsa


def sample_attention(r, li, page_table, cache_mla_ckv, cache_mla_kpe, cache_moba_kv,
                     cache_diff_kv, cache_nsa_kv, win_buf, lw):
    tq = r['ckv'].shape[1]
    past_len = page_table.shape[1] * cache_mla_ckv.shape[2]
    qpos = past_len + jnp.arange(tq)
    kpos = jnp.arange(past_len + tq)
    w_eff = win_buf.shape[1]
    wpos = past_len - w_eff + jnp.arange(w_eff + tq)
    lam = diff_lambda(lw['diff_lambda'], li)

    def with_past(pool, pt, new):
        rows = pool[li, pt]
        rows = rows.reshape((rows.shape[0], rows.shape[1] * rows.shape[2]) + rows.shape[3:])
        return jnp.concatenate([rows, new], axis=1)

    def chunk(pt, q_mla, ckv_new, kpe_new, q_moba, moba_new, q_diff, diff_new,
              q_nsa, g_nsa, nsa_new, wbuf, wnew):
        c = pt.shape[0]
        k_mla, v_mla = mla_expand(with_past(cache_mla_ckv, pt, ckv_new),
                                  with_past(cache_mla_kpe, pt, kpe_new), kpos, lw)
        o_mla = attend(q_mla[:, :, :, None], k_mla, v_mla, causal_mask(qpos, kpos))[:, :, :, 0]
        mkv = with_past(cache_moba_kv, pt, moba_new)
        o_moba = moba_attn(q_moba, mkv[:, :, 0], mkv[:, :, 1], qpos)
        dkv = with_past(cache_diff_kv, pt, diff_new)
        o_diff = diff_attn(q_diff, dkv[:, :, 0].reshape(c, -1, DIFF_KV_HEADS, 2, DIFF_DIM),
                           dkv[:, :, 1], qpos, kpos, lam)
        nkv = with_past(cache_nsa_kv, pt, nsa_new)
        kc, vc, cend = nsa_compress(nkv[:, :, 0], nkv[:, :, 1], lw)
        o_cs = nsa_attn(q_nsa, g_nsa, kc, vc, cend, nkv[:, :, 2], nkv[:, :, 3], qpos)
        kw = jnp.concatenate([wbuf, wnew], axis=1)
        o_win = attend(q_nsa, kw[:, :, 0], kw[:, :, 1], window_mask(qpos, wpos))
        return o_mla, o_moba, o_diff, o_cs + g_nsa[..., 2:3] * o_win

    return sweep_batch(chunk, (page_table, r['q_mla'], r['ckv'], r['kpe'], r['q_moba'], r['moba_kv'],
                               r['q_diff'], r['diff_kv'], r['q_nsa'], r['g_nsa'], r['nsa_kv'],
                               win_buf, r['nsa_win']), DEC_CHUNK)


def merge_and_ffn(x, o_mla, o_moba, o_diff, o_nsa, g_merge, lw, li):
    b, t, _ = x.shape
    o_diff = rmsnorm(o_diff.reshape(b, t, DIFF_HEADS, DIFF_VDIM), lw['diff_subln_g']) * (1.0 - lambda_init(li))
    branches = jnp.stack([o.reshape(b, t, MIX_WIDTH) for o in (o_mla, o_moba, o_diff, o_nsa)], axis=2)
    y = jnp.einsum('btnw,nwd->btnd', branches, lw['w_branch'])
    x = x + jnp.sum(g_merge * y, axis=2) @ lw['w_out']
    h = rmsnorm(x, lw['ln_ffn_g'])
    gate, up = jnp.split(h @ lw['ffn_w_gu'], 2, axis=-1)
    return x + (jax.nn.silu(gate) * up) @ lw['ffn_w_down']


def setup_inputs(seed: int = 0) -> dict:
    key = jax.random.key(seed)
    keys = iter(jax.random.split(key, 48))
    f32 = jnp.float32

    def normal(shape, scale):
        return jax.random.normal(next(keys), shape, f32) * scale

    def gain(shape):
        return 1.0 + normal(shape, 0.05)

    n_pages = PAST_LEN // PAGE_SIZE
    n_pool = (5 * DEC_BATCH * n_pages) // 4
    w_eff = min(NSA_WINDOW, PAST_LEN)
    L = DEPTH
    x_prompt = normal((BATCH, SEQ, D_MODEL), 1.0)
    x_sample = normal((DEC_BATCH, DEC_SEQ, D_MODEL), 1.0)
    cache_mla_ckv = normal((L, n_pool, PAGE_SIZE, MLA_KV_LORA), 1.0)
    cache_mla_kpe = normal((L, n_pool, PAGE_SIZE, MLA_ROPE), 1.0)
    cache_moba_kv = normal((L, n_pool, PAGE_SIZE, 2, MOBA_KV_HEADS, HEAD_DIM), 1.0)
    cache_diff_kv = normal((L, n_pool, PAGE_SIZE, 2, DIFF_KV_HEADS, DIFF_VDIM), 1.0)
    cache_nsa_kv = normal((L, n_pool, PAGE_SIZE, 4, NSA_KV_HEADS, HEAD_DIM), 1.0)
    state_nsa_win = normal((L, DEC_BATCH, w_eff, 2, NSA_KV_HEADS, HEAD_DIM), 1.0)
    page_table = jax.random.permutation(next(keys), n_pool)[: DEC_BATCH * n_pages]
    page_table = page_table.reshape(DEC_BATCH, n_pages).astype(jnp.int32)
    return {
        'x_prompt': x_prompt,
        'x_sample': x_sample,
        'cache_mla_ckv': cache_mla_ckv,
        'cache_mla_kpe': cache_mla_kpe,
        'cache_moba_kv': cache_moba_kv,
        'cache_diff_kv': cache_diff_kv,
        'cache_nsa_kv': cache_nsa_kv,
        'state_nsa_win': state_nsa_win,
        'page_table': page_table,
        'ln_attn_g': gain((L, D_MODEL)),
        'w_in': normal((L, D_MODEL, IN_DIM), D_MODEL ** -0.5),
        'mla_cq_g': gain((L, MLA_Q_LORA)),
        'mla_ckv_g': gain((L, MLA_KV_LORA)),
        'mla_w_uq': normal((L, MLA_Q_LORA, MLA_HEADS, MLA_QK), MLA_Q_LORA ** -0.5),
        'mla_qn_g': gain((L, MLA_QK)),
        'mla_w_uk': normal((L, MLA_KV_LORA, MLA_HEADS, MLA_NOPE), MLA_KV_LORA ** -0.5),
        'mla_w_uv': normal((L, MLA_KV_LORA, MLA_HEADS, MLA_V), MLA_KV_LORA ** -0.5),
        'mla_kn_g': gain((L, MLA_QK)),
        'moba_qn_g': gain((L, HEAD_DIM)),
        'moba_kn_g': gain((L, HEAD_DIM)),
        'diff_qn_g': gain((L, DIFF_DIM)),
        'diff_kn_g': gain((L, DIFF_DIM)),
        'diff_lambda': normal((L, 4, DIFF_DIM), 0.1),
        'diff_subln_g': gain((L, DIFF_VDIM)),
        'nsa_qn_g': gain((L, HEAD_DIM)),
        'nsa_kn_g': gain((L, 3, HEAD_DIM)),
        'nsa_cmp_pos': normal((L, 2, NSA_CMP_LEN, HEAD_DIM), 0.5),
        'nsa_cmp_w1': normal((L, 2, NSA_CMP_LEN * HEAD_DIM, NSA_CMP_HIDDEN), (NSA_CMP_LEN * HEAD_DIM) ** -0.5),
        'nsa_cmp_b1': normal((L, 2, NSA_CMP_HIDDEN), 0.02),
        'nsa_cmp_w2': normal((L, 2, NSA_CMP_HIDDEN, HEAD_DIM), NSA_CMP_HIDDEN ** -0.5),
        'w_branch': normal((L, N_BRANCH, MIX_WIDTH, D_MODEL), MIX_WIDTH ** -0.5),
        'w_out': normal((L, D_MODEL, D_MODEL), D_MODEL ** -0.5),
        'ln_ffn_g': gain((L, D_MODEL)),
        'ffn_w_gu': normal((L, D_MODEL, 2 * FFN_HIDDEN), D_MODEL ** -0.5),
        'ffn_w_down': normal((L, FFN_HIDDEN, D_MODEL), FFN_HIDDEN ** -0.5),
    }


def reference(x_prompt, x_sample, cache_mla_ckv, cache_mla_kpe, cache_moba_kv, cache_diff_kv,
              cache_nsa_kv, state_nsa_win, page_table, ln_attn_g, w_in, mla_cq_g, mla_ckv_g,
              mla_w_uq, mla_qn_g, mla_w_uk, mla_w_uv, mla_kn_g, moba_qn_g, moba_kn_g, diff_qn_g,
              diff_kn_g, diff_lambda, diff_subln_g, nsa_qn_g, nsa_kn_g, nsa_cmp_pos, nsa_cmp_w1,
              nsa_cmp_b1, nsa_cmp_w2, w_branch, w_out, ln_ffn_g, ffn_w_gu, ffn_w_down):
    t_p = x_prompt.shape[1]
    t_s = x_sample.shape[1]
    past_len = page_table.shape[1] * cache_mla_ckv.shape[2]
    pos_p = jnp.arange(t_p)
    pos_s = past_len + jnp.arange(t_s)
    xp, xs = x_prompt, x_sample
    ckv_p, ckv_s, kpe_p, kpe_s = [], [], [], []
    moba_p, moba_s, diff_p, diff_s = [], [], [], []
    nsa_p, nsa_s, win_p, win_s = [], [], [], []
    for li in range(DEPTH):
        lw = {
            'ln_attn_g': ln_attn_g[li], 'w_in': w_in[li],
            'mla_cq_g': mla_cq_g[li], 'mla_ckv_g': mla_ckv_g[li], 'mla_w_uq': mla_w_uq[li],
            'mla_qn_g': mla_qn_g[li], 'mla_w_uk': mla_w_uk[li], 'mla_w_uv': mla_w_uv[li],
            'mla_kn_g': mla_kn_g[li], 'moba_qn_g': moba_qn_g[li], 'moba_kn_g': moba_kn_g[li],
            'diff_qn_g': diff_qn_g[li], 'diff_kn_g': diff_kn_g[li], 'diff_lambda': diff_lambda[li],
            'diff_subln_g': diff_subln_g[li], 'nsa_qn_g': nsa_qn_g[li], 'nsa_kn_g': nsa_kn_g[li],
            'nsa_cmp_pos': nsa_cmp_pos[li], 'nsa_cmp_w1': nsa_cmp_w1[li], 'nsa_cmp_b1': nsa_cmp_b1[li],
            'nsa_cmp_w2': nsa_cmp_w2[li], 'w_branch': w_branch[li], 'w_out': w_out[li],
            'ln_ffn_g': ln_ffn_g[li], 'ffn_w_gu': ffn_w_gu[li], 'ffn_w_down': ffn_w_down[li],
        }
        rp = token_rows(xp, pos_p, lw)
        rs = token_rows(xs, pos_s, lw)
        op = prompt_attention(rp, pos_p, lw, li)
        osm = sample_attention(rs, li, page_table, cache_mla_ckv, cache_mla_kpe, cache_moba_kv,
                               cache_diff_kv, cache_nsa_kv, state_nsa_win[li], lw)
        xp = merge_and_ffn(xp, op[0], op[1], op[2], op[3], rp['g_merge'], lw, li)
        xs = merge_and_ffn(xs, osm[0], osm[1], osm[2], osm[3], rs['g_merge'], lw, li)
        ckv_p.append(rp['ckv'])
        ckv_s.append(rs['ckv'])
        kpe_p.append(rp['kpe'])
        kpe_s.append(rs['kpe'])
        moba_p.append(rp['moba_kv'])
        moba_s.append(rs['moba_kv'])
        diff_p.append(rp['diff_kv'])
        diff_s.append(rs['diff_kv'])
        nsa_p.append(rp['nsa_kv'])
        nsa_s.append(rs['nsa_kv'])
        win_p.append(rp['nsa_win'][:, -min(NSA_WINDOW, t_p):])
        w_all = jnp.concatenate([state_nsa_win[li], rs['nsa_win']], axis=1)
        win_s.append(w_all[:, -min(NSA_WINDOW, w_all.shape[1]):])
    return (xp, xs,
            jnp.stack(ckv_p), jnp.stack(ckv_s), jnp.stack(kpe_p), jnp.stack(kpe_s),
            jnp.stack(moba_p), jnp.stack(moba_s), jnp.stack(diff_p), jnp.stack(diff_s),
            jnp.stack(nsa_p), jnp.stack(nsa_s), jnp.stack(win_p), jnp.stack(win_s))
```

```python
import functools
import math

import jax
import jax.numpy as jnp
from jax import lax
from jax.experimental import pallas as pl
from jax.experimental.pallas import tpu as pltpu

F32 = jnp.float32
BF16 = jnp.bfloat16

EPS = 1e-6
NEG = -1e30
LANE = 128
VMEM_LIMIT = 56 * 1024 * 1024

HEAD_DIM = 64
ROT_DIM = 16
ROPE_THETA = 500000.0
MLA_HEADS = 8
MLA_NOPE = 64
MLA_ROPE = 32
MLA_QK = MLA_NOPE + MLA_ROPE
MLA_THETA = 10000.0
MOBA_BLOCK = 256
MOBA_TOPK = 3
NSA_CMP_LEN = 32
NSA_CMP_STRIDE = 16
NSA_SEL_BLOCK = 64
NSA_TOPN = 16
NSA_WINDOW = 512
SEL_FORCE = 1e30
N_HEADS = 8
Q_BLOCK = 128
PAGES_PER_STEP = 16


def _lambda_init(li):
    return 0.8 - 0.6 * math.exp(-0.3 * li)


def _tile(n, target, mult):
    best = None
    d = mult
    while d <= min(n, target):
        if n % d == 0:
            best = d
        d += mult
    return best if best is not None else n


def _params(sem):
    return pltpu.CompilerParams(dimension_semantics=sem, vmem_limit_bytes=VMEM_LIMIT)


def _dot(a, b):
    return jnp.dot(a.astype(BF16), b.astype(BF16), preferred_element_type=F32)


def _dot_nt(a, b):
    return lax.dot_general(a.astype(BF16), b.astype(BF16), (((1,), (1,)), ((), ())),
                           preferred_element_type=F32)


def _split(a):
    hi = a.astype(BF16)
    lo = (a - hi.astype(F32)).astype(BF16)
    return hi, lo


def _dot_x(a, b_exact):
    hi, lo = _split(a)
    b = b_exact.astype(BF16)
    return (jnp.dot(hi, b, preferred_element_type=F32)
            + jnp.dot(lo, b, preferred_element_type=F32))


def _dot_xx(a, b):
    ah, al = _split(a)
    bh, bl = _split(b)
    return (jnp.dot(ah, bh, preferred_element_type=F32)
            + jnp.dot(ah, bl, preferred_element_type=F32)
            + jnp.dot(al, bh, preferred_element_type=F32))


def _rms(x, g):
    r = lax.rsqrt(jnp.mean(x * x, axis=-1, keepdims=True) + EPS)
    return x * r * g


def _mm_norm_kernel(x_ref, g_ref, w_ref, o_ref, h_sc):
    @pl.when(pl.program_id(1) == 0)
    def _():
        h_sc[...] = _rms(x_ref[...], g_ref[...]).astype(BF16)

    o_ref[...] = jnp.dot(h_sc[...], w_ref[...], preferred_element_type=F32)


def mm_norm(x, g, w):
    m, k = x.shape
    n = w.shape[1]
    tm = _tile(m, 1024, 8)
    tn = _tile(n, 512, LANE)
    return pl.pallas_call(
        _mm_norm_kernel,
        grid=(m // tm, n // tn),
        in_specs=[pl.BlockSpec((tm, k), lambda i, j: (i, 0)),
                  pl.BlockSpec((1, k), lambda i, j: (0, 0)),
                  pl.BlockSpec((k, tn), lambda i, j: (0, j))],
        out_specs=pl.BlockSpec((tm, tn), lambda i, j: (i, j)),
        out_shape=jax.ShapeDtypeStruct((m, n), F32),
        scratch_shapes=[pltpu.VMEM((tm, k), BF16)],
        compiler_params=_params(("parallel", "arbitrary")),
        name="mm_norm",
    )(x, g.reshape(1, k), w)


def _swiglu_kernel(x_ref, g_ref, wg_ref, wu_ref, o_ref, h_sc):
    @pl.when(pl.program_id(1) == 0)
    def _():
        h_sc[...] = _rms(x_ref[...], g_ref[...]).astype(BF16)

    h = h_sc[...]
    gate = jnp.dot(h, wg_ref[...], preferred_element_type=F32)
    up = jnp.dot(h, wu_ref[...], preferred_element_type=F32)
    o_ref[...] = (gate * jax.nn.sigmoid(gate) * up).astype(o_ref.dtype)


def mm_norm_swiglu(x, g, w_gu):
    m, k = x.shape
    hid = w_gu.shape[1] // 2
    tm = _tile(m, 1024, 8)
    tn = _tile(hid, 512, LANE)
    nj = hid // tn
    return pl.pallas_call(
        _swiglu_kernel,
        grid=(m // tm, nj),
        in_specs=[pl.BlockSpec((tm, k), lambda i, j: (i, 0)),
                  pl.BlockSpec((1, k), lambda i, j: (0, 0)),
                  pl.BlockSpec((k, tn), lambda i, j: (0, j)),
                  pl.BlockSpec((k, tn), lambda i, j: (0, j + nj))],
        out_specs=pl.BlockSpec((tm, tn), lambda i, j: (i, j)),
        out_shape=jax.ShapeDtypeStruct((m, hid), BF16),
        scratch_shapes=[pltpu.VMEM((tm, k), BF16)],
        compiler_params=_params(("parallel", "arbitrary")),
        name="ffn_gate_up",
    )(x, g.reshape(1, k), w_gu, w_gu)


def _mm_res_kernel(a_ref, w_ref, r_ref, o_ref):
    o_ref[...] = r_ref[...] + jnp.dot(a_ref[...], w_ref[...], preferred_element_type=F32)


def mm_res(a, w, res):
    m, k = a.shape
    n = w.shape[1]
    tm = _tile(m, 512, 8)
    tn = _tile(n, 512, LANE)
    return pl.pallas_call(
        _mm_res_kernel,
        grid=(m // tm, n // tn),
        in_specs=[pl.BlockSpec((tm, k), lambda i, j: (i, 0)),
                  pl.BlockSpec((k, tn), lambda i, j: (0, j)),
                  pl.BlockSpec((tm, tn), lambda i, j: (i, j))],
        out_specs=pl.BlockSpec((tm, tn), lambda i, j: (i, j)),
        out_shape=jax.ShapeDtypeStruct((m, n), F32),
        compiler_params=_params(("parallel", "arbitrary")),
        name="mm_res",
    )(a, w, res)


def _merge_kernel(sub_scale, x_ref, g_ref, oa_ref, ob_ref, oc_ref, od_ref, sg_ref,
                  wg0, wg1, wg2, wg3, wb0, wb1, wb2, wb3, y_ref, h_sc, o_sc):
    @pl.when(pl.program_id(1) == 0)
    def _():
        h_sc[...] = _rms(x_ref[...], g_ref[...]).astype(BF16)
        o_sc[0] = oa_ref[...].astype(BF16)
        o_sc[1] = ob_ref[...].astype(BF16)
        oc = oc_ref[...]
        w = oc.shape[1]
        for c in range(w // LANE):
            sl = slice(c * LANE, (c + 1) * LANE)
            o_sc[2, :, sl] = (_rms(oc[:, sl], sg_ref[...]) * sub_scale).astype(BF16)
        o_sc[3] = od_ref[...].astype(BF16)

    h = h_sc[...]
    acc = None
    for b, (wg, wb) in enumerate(((wg0, wb0), (wg1, wb1), (wg2, wb2), (wg3, wb3))):
        gate = jax.nn.sigmoid(jnp.dot(h, wg[...], preferred_element_type=F32))
        yb = jnp.dot(o_sc[b], wb[0], preferred_element_type=F32)
        acc = gate * yb if acc is None else acc + gate * yb
    y_ref[...] = acc.astype(y_ref.dtype)


def merge_branches(x, g, outs, subln_g, sub_scale, w_gate, w_branch):
    m, d = x.shape
    wmix = outs[0].shape[1]
    tm = _tile(m, 512, 8)
    tn = _tile(d, 256, LANE)
    nj = d // tn
    o_spec = pl.BlockSpec((tm, wmix), lambda i, j: (i, 0))
    in_specs = [pl.BlockSpec((tm, d), lambda i, j: (i, 0)),
                pl.BlockSpec((1, d), lambda i, j: (0, 0)),
                o_spec, o_spec, o_spec, o_spec,
                pl.BlockSpec((1, LANE), lambda i, j: (0, 0))]
    for b in range(4):
        in_specs.append(pl.BlockSpec((d, tn), functools.partial(lambda i, j, b: (0, b * nj + j), b=b)))
    for b in range(4):
        in_specs.append(pl.BlockSpec((1, wmix, tn), functools.partial(lambda i, j, b: (b, 0, j), b=b)))
    return pl.pallas_call(
        functools.partial(_merge_kernel, sub_scale),
        grid=(m // tm, nj),
        in_specs=in_specs,
        out_specs=pl.BlockSpec((tm, tn), lambda i, j: (i, j)),
        out_shape=jax.ShapeDtypeStruct((m, d), BF16),
        scratch_shapes=[pltpu.VMEM((tm, d), BF16), pltpu.VMEM((4, tm, wmix), BF16)],
        compiler_params=_params(("parallel", "arbitrary")),
        name="merge_branches",
    )(x, g.reshape(1, d), outs[0], outs[1], outs[2], outs[3], subln_g.reshape(1, LANE),
      w_gate, w_gate, w_gate, w_gate, w_branch, w_branch, w_branch, w_branch)


Z_CQ, Z_CKV, Z_SLAB, Z_SLAB_W = 0, 384, 512, 2688
Z_LAST = Z_SLAB + Z_SLAB_W
Z_WIDTH = Z_LAST + LANE
SLAB_GROUPS = (("q_moba", 512, (1,) * 8), ("moba_kv", 256, (1, 1, 0, 0)),
               ("q_diff", 512, (1,) * 8), ("diff_kv", 512, (1, 1, 1, 1, 0, 0, 0, 0)),
               ("q_nsa", 512, (1,) * 8), ("nsa_kv", 256, (0, 0, 1, 0)), ("nsa_win", 128, (1, 0)))
SLAB_ENABLE = sum((g[2] for g in SLAB_GROUPS), ())


def _rope_apply(y, c, s1, s2, half):
    return y * c + pltpu.roll(y, LANE - half, 1) * s1 + pltpu.roll(y, half, 1) * s2


def _rows_kernel(z_ref, cqg_ref, ckvg_ref, wuq_ref, b768_ref, qg_ref, wqa_ref, slabg_ref,
                 c64_ref, s64a_ref, s64b_ref, cm_ref, sma_ref, smb_ref,
                 qa_ref, qpe_ref, ckv_ref, last_ref, qmoba_ref, mkv_ref, qdiff_ref, dkv_ref,
                 qnsa_ref, nkv_ref, nwin_ref):
    mla_scale = MLA_QK ** -0.5
    cq = _rms(z_ref[:, Z_CQ:Z_CQ + 384], cqg_ref[...])
    q = jnp.dot(cq.astype(BF16), wuq_ref[...], preferred_element_type=F32)
    ss = _dot_x(q * q, b768_ref[...])
    q = q * lax.rsqrt(ss * (1.0 / MLA_QK) + EPS) * qg_ref[...]
    for c in range(2):
        sl = slice(c * LANE, (c + 1) * LANE)
        qr = _rope_apply(q[:, sl], cm_ref[...], sma_ref[...], smb_ref[...], MLA_ROPE // 2)
        qpe_ref[:, sl] = qr * mla_scale
    qa_ref[...] = jnp.dot(q[:, 256:].astype(BF16), wqa_ref[...], preferred_element_type=F32) * mla_scale
    ckv_ref[...] = _rms(z_ref[:, Z_CKV:Z_CKV + LANE], ckvg_ref[...])
    lane = lax.broadcasted_iota(jnp.int32, (1, LANE), 1)
    zl = z_ref[:, Z_LAST:Z_LAST + LANE]
    last_ref[...] = jnp.where(lane < MLA_ROPE, zl, jax.nn.sigmoid(zl))
    ri = lax.broadcasted_iota(jnp.int32, (LANE, LANE), 0) // HEAD_DIM
    ci = lax.broadcasted_iota(jnp.int32, (LANE, LANE), 1) // HEAD_DIM
    b128 = (ri == ci).astype(BF16)
    outs = []
    for (name, width, _), ref in zip(SLAB_GROUPS, (qmoba_ref, mkv_ref, qdiff_ref, dkv_ref,
                                                  qnsa_ref, nkv_ref, nwin_ref)):
        outs += [(ref, k, name.startswith("q_")) for k in range(width // LANE)]
    for c, (ref, k, is_q) in enumerate(outs):
        x = z_ref[:, Z_SLAB + c * LANE:Z_SLAB + (c + 1) * LANE]
        e0, e1 = SLAB_ENABLE[2 * c], SLAB_ENABLE[2 * c + 1]
        if e0 or e1:
            ss = _dot_x(x * x, b128)
            y = x * lax.rsqrt(ss * (1.0 / HEAD_DIM) + EPS) * slabg_ref[:, c * LANE:(c + 1) * LANE]
            y = _rope_apply(y, c64_ref[...], s64a_ref[...], s64b_ref[...], ROT_DIM // 2)
            if not (e0 and e1):
                y = jnp.where((lane < HEAD_DIM) if e0 else (lane >= HEAD_DIM), y, x)
        else:
            y = x
        if is_q:
            y = y * (HEAD_DIM ** -0.5)
        ref[:, k * LANE:(k + 1) * LANE] = y


def token_rows(z, lw, tabs):
    m = z.shape[0]
    tm = _tile(m, 256, 8)
    row = lambda w: pl.BlockSpec((tm, w), lambda i: (i, 0))
    full = lambda a: pl.BlockSpec(a.shape, lambda i: (0,) * a.ndim)
    consts = (lw["cq_g"], lw["ckv_g"], lw["wuq"], lw["b768"], lw["q_g768"], lw["wqa"], lw["slab_g"])
    out_w = (("qa", 1024), ("qpe", 256), ("ckv", 128), ("last", 128), ("q_moba", 512), ("moba_kv", 256),
             ("q_diff", 512), ("diff_kv", 512), ("q_nsa", 512), ("nsa_kv", 256), ("nsa_win", 128))
    outs = pl.pallas_call(
        _rows_kernel,
        grid=(m // tm,),
        in_specs=[row(Z_WIDTH)] + [full(a) for a in consts] + [row(LANE)] * 6,
        out_specs=[row(w) for _, w in out_w],
        out_shape=[jax.ShapeDtypeStruct((m, w), F32) for _, w in out_w],
        compiler_params=_params(("parallel",)),
        name="token_rows",
    )(z, *consts, *tabs)
    return dict(zip((n for n, _ in out_w), outs))


def _flash_init(m_sc, l_sc, acc_sc):
    m_sc[...] = jnp.full(m_sc.shape, NEG, F32)
    l_sc[...] = jnp.zeros(l_sc.shape, F32)
    acc_sc[...] = jnp.zeros(acc_sc.shape, F32)


def _masked(x, mask, fill):
    if mask.shape[0] == x.shape[0]:
        return jnp.where(mask, x, fill)
    nq, t = mask.shape
    return jnp.where(mask[None], x.reshape(x.shape[0] // nq, nq, t), fill).reshape(x.shape)


def _flash_update(s, mask, pv, m_sc, l_sc, acc_sc, idx=None):
    at = (lambda r: r.at[idx]) if idx is not None else (lambda r: r)
    if mask is not None:
        s = _masked(s, mask, NEG)
    m_prev = at(m_sc)[...]
    m_new = jnp.maximum(m_prev, jnp.max(s, axis=-1, keepdims=True))
    alpha = jnp.exp(m_prev - m_new)
    p = jnp.exp(s - m_new)
    if mask is not None:
        p = _masked(p, mask, 0.0)
    at(l_sc)[...] = alpha * at(l_sc)[...] + jnp.sum(p, axis=-1, keepdims=True)
    at(acc_sc)[...] = alpha * at(acc_sc)[...] + pv(p)
    at(m_sc)[...] = m_new


def _flash_out(l_sc, acc_sc, idx=None):
    at = (lambda r: r.at[idx]) if idx is not None else (lambda r: r)
    return at(acc_sc)[...] / jnp.maximum(at(l_sc)[...], 1e-30)


def _row_tok(nrows, nq):
    return lax.broadcasted_iota(jnp.int32, (nrows, 1), 0) % nq


def _cat(vals, axis):
    return vals[0] if len(vals) == 1 else jnp.concatenate(vals, axis=axis)


class _Geom:
    def __init__(self, sample, nq, n_qb, tok_per_ref, n_ref, n_chunk, past_len):
        self.sample = sample
        self.nq = nq
        self.n_qb = n_qb
        self.tok_per_ref = tok_per_ref
        self.n_ref = n_ref
        self.tc = tok_per_ref * n_ref
        self.n_chunk = n_chunk
        self.past_len = past_len

    def q_base(self, qb):
        return self.past_len if self.sample else qb * self.nq

    def chunk_live(self, qb, c):
        if self.sample:
            return c >= 0
        return c * self.tc <= qb * self.nq + self.nq - 1

    def last_chunk(self, qb):
        if self.sample:
            return self.n_chunk - 1
        return jnp.minimum((qb * self.nq + self.nq - 1) // self.tc, self.n_chunk - 1)


def _mla_kernel(geom, pt_ref, qa_ref, qpe_ref, *refs):
    n = geom.n_ref
    ckv_refs, kpe_refs = refs[:n], refs[n:2 * n]
    refs = refs[2 * n:]
    if geom.sample:
        ckvn_ref, kpen_ref, cosn_ref, sinn_ref = refs[:4]
        refs = refs[4:]
    cos_ref, sin_ref, wukt_ref, gpe_ref, wuv_ref, o_ref, m_sc, l_sc, acc_sc = refs
    nq, tc = geom.nq, geom.tc
    nrows = MLA_HEADS * nq
    qb, c = pl.program_id(1), pl.program_id(2)
    qpos = geom.q_base(qb) + _row_tok(nrows, nq)

    def process(ckv, kpet, cos, sin, kpos0, masked):
        t = ckv.shape[0]
        ckv_b = ckv.astype(BF16)
        knt = lax.dot_general(wukt_ref[...], ckv_b, (((1,), (1,)), ((), ())),
                              preferred_element_type=F32)
        ss = jnp.sum((knt * knt).reshape(MLA_HEADS, MLA_NOPE, t), axis=1)
        pe2 = jnp.sum(kpet * kpet, axis=0, keepdims=True)
        rt = lax.rsqrt((ss + pe2) * (1.0 / MLA_QK) + EPS)
        kg = kpet * gpe_ref[...]
        x1, x2 = kg[:MLA_ROPE // 2], kg[MLA_ROPE // 2:]
        kr = jnp.concatenate([x1 * cos - x2 * sin, x2 * cos + x1 * sin,
                              jnp.zeros((LANE - MLA_ROPE, t), F32)], axis=0)
        s = _dot_nt(qa_ref[0, 0], ckv_b) + _dot(qpe_ref[0, 0], kr)
        s = jnp.concatenate([s[h * nq:(h + 1) * nq] * rt[h:h + 1] for h in range(MLA_HEADS)], axis=0)
        mask = None
        if masked:
            kpos = kpos0 + lax.broadcasted_iota(jnp.int32, (1, t), 1)
            mask = kpos <= qpos
        _flash_update(s, mask, lambda p: _dot(p, ckv_b), m_sc, l_sc, acc_sc)

    @pl.when(c == 0)
    def _():
        _flash_init(m_sc, l_sc, acc_sc)
        if geom.sample:
            process(ckvn_ref[0], kpen_ref[0], cosn_ref[...], sinn_ref[...], geom.past_len, True)

    @pl.when(geom.chunk_live(qb, c))
    def _():
        ckv = _cat([r[0, 0] if geom.sample else r[0] for r in ckv_refs], 0)
        kpet = _cat([r[0, 0] if geom.sample else r[0] for r in kpe_refs], 1)
        process(ckv, kpet, cos_ref[...], sin_ref[...], c * tc, not geom.sample)

    @pl.when(c == geom.n_chunk - 1)
    def _():
        lat = _flash_out(l_sc, acc_sc)
        for h in range(MLA_HEADS):
            o_ref[0, 0, h * nq:(h + 1) * nq, :] = _dot(lat[h * nq:(h + 1) * nq], wuv_ref[h])


def _chunk_idx(geom, qb, c):
    return c if geom.sample else jnp.minimum(c, geom.last_chunk(qb))


def mla_attention(geom, li, pt, qa, qpe, ckv_src, kpet_src, new, tabs, lw):
    b, n_qb, nrows, _ = qa.shape
    n, tc, tpr = geom.n_ref, geom.tc, geom.tok_per_ref
    cos_t, sin_t, cos_n, sin_n = tabs
    qspec = pl.BlockSpec((1, 1, nrows, LANE), lambda i, q, c, pt: (i, q, 0, 0))
    in_specs = [qspec, qspec]
    args = [qa, qpe]
    if geom.sample:
        for j in range(n):
            in_specs.append(pl.BlockSpec((1, 1, tpr, LANE),
                                         functools.partial(lambda i, q, c, pt, j: (li, pt[i, c * n + j], 0, 0), j=j)))
        for j in range(n):
            in_specs.append(pl.BlockSpec((1, 1, MLA_ROPE, tpr),
                                         functools.partial(lambda i, q, c, pt, j: (li, pt[i, c * n + j], 0, 0), j=j)))
        args += [ckv_src] * n + [kpet_src] * n
        in_specs += [pl.BlockSpec((1, LANE, LANE), lambda i, q, c, pt: (i, 0, 0)),
                     pl.BlockSpec((1, MLA_ROPE, LANE), lambda i, q, c, pt: (i, 0, 0)),
                     pl.BlockSpec((MLA_ROPE // 2, LANE), lambda i, q, c, pt: (0, 0)),
                     pl.BlockSpec((MLA_ROPE // 2, LANE), lambda i, q, c, pt: (0, 0))]
        args += [new[0], new[1], cos_n, sin_n]
    else:
        in_specs += [pl.BlockSpec((1, tc, LANE), lambda i, q, c, pt: (i, _chunk_idx(geom, q, c), 0)),
                     pl.BlockSpec((1, MLA_ROPE, tc), lambda i, q, c, pt: (i, 0, _chunk_idx(geom, q, c)))]
        args += [ckv_src, kpet_src]
    tab_spec = pl.BlockSpec((MLA_ROPE // 2, tc), lambda i, q, c, pt: (0, _chunk_idx(geom, q, c)))
    in_specs += [tab_spec, tab_spec,
                 pl.BlockSpec(lw["wukt"].shape, lambda i, q, c, pt: (0, 0)),
                 pl.BlockSpec((MLA_ROPE, 1), lambda i, q, c, pt: (0, 0)),
                 pl.BlockSpec(lw["wuv"].shape, lambda i, q, c, pt: (0, 0, 0))]
    args += [cos_t, sin_t, lw["wukt"], lw["k_gpe"], lw["wuv"]]
    return pl.pallas_call(
        functools.partial(_mla_kernel, geom),
        grid_spec=pltpu.PrefetchScalarGridSpec(
            num_scalar_prefetch=1, grid=(b, n_qb, geom.n_chunk), in_specs=in_specs,
            out_specs=pl.BlockSpec((1, 1, nrows, HEAD_DIM), lambda i, q, c, pt: (i, q, 0, 0)),
            scratch_shapes=[pltpu.VMEM((nrows, 1), F32), pltpu.VMEM((nrows, 1), F32),
                            pltpu.VMEM((nrows, LANE), F32)]),
        out_shape=jax.ShapeDtypeStruct((b, n_qb, nrows, HEAD_DIM), F32),
        compiler_params=_params(("arbitrary", "arbitrary", "arbitrary")),
        name="mla_sample" if geom.sample else "mla_prompt",
    )(pt, *args)


def _diff_kernel(geom, lam_init, pt_ref, q_ref, *refs):
    n = geom.n_ref
    kv_refs = refs[:n]
    refs = refs[n:]
    if geom.sample:
        new_ref = refs[0]
        refs = refs[1:]
    lam_ref, o_ref, m_sc, l_sc, acc_sc = refs
    nq, tc, tpr = geom.nq, geom.tc, geom.tok_per_ref
    nrows = 4 * nq
    qb, c = pl.program_id(1), pl.program_id(2)
    qpos = geom.q_base(qb) + _row_tok(nrows, nq)

    def rows_of(ref, sample, ntok, k):
        if sample:
            return ref[0, 0, pl.ds(k, ntok, stride=4), :]
        return ref[0, pl.ds(k, ntok, stride=4), :]

    def process(get, t, kpos0, masked):
        mask = None
        if masked:
            kpos = kpos0 + lax.broadcasted_iota(jnp.int32, (1, t), 1)
            mask = kpos <= qpos
        for g in range(2):
            k_b = get(g).astype(BF16)
            v_b = get(2 + g).astype(BF16)
            s = _dot_nt(q_ref[0, 0, g], k_b)
            _flash_update(s, mask, lambda p: _dot(p, v_b), m_sc, l_sc, acc_sc, idx=g)

    @pl.when(c == 0)
    def _():
        _flash_init(m_sc, l_sc, acc_sc)
        if geom.sample:
            process(lambda k: new_ref[0, pl.ds(k, LANE, stride=4), :], LANE, geom.past_len, True)

    @pl.when(geom.chunk_live(qb, c))
    def _():
        process(lambda k: _cat([rows_of(r, geom.sample, tpr, k) for r in kv_refs], 0),
                tc, c * tc, not geom.sample)

    @pl.when(c == geom.n_chunk - 1)
    def _():
        lv = lam_ref[...]
        lam = (jnp.exp(jnp.sum(lv[0:1] * lv[1:2], axis=-1, keepdims=True))
               - jnp.exp(jnp.sum(lv[2:3] * lv[3:4], axis=-1, keepdims=True)) + lam_init)
        for g in range(2):
            o = _flash_out(l_sc, acc_sc, idx=g)
            o_ref[0, 0, g] = o[:2 * nq] - lam * o[2 * nq:]


def diff_attention(geom, li, pt, q, kv_src, new, lam_vecs):
    b, n_qb = q.shape[:2]
    nq, n, tc, tpr = geom.nq, geom.n_ref, geom.tc, geom.tok_per_ref
    in_specs = [pl.BlockSpec((1, 1, 2, 4 * nq, LANE), lambda i, qq, c, pt: (i, qq, 0, 0, 0))]
    args = [q]
    if geom.sample:
        for j in range(n):
            in_specs.append(pl.BlockSpec((1, 1, 4 * tpr, LANE),
                                         functools.partial(lambda i, qq, c, pt, j: (li, pt[i, c * n + j], 0, 0), j=j)))
        args += [kv_src] * n
        in_specs.append(pl.BlockSpec((1, 4 * LANE, LANE), lambda i, qq, c, pt: (i, 0, 0)))
        args.append(new)
    else:
        in_specs.append(pl.BlockSpec((1, 4 * tc, LANE), lambda i, qq, c, pt: (i, _chunk_idx(geom, qq, c), 0)))
        args.append(kv_src)
    in_specs.append(pl.BlockSpec((4, HEAD_DIM), lambda i, qq, c, pt: (0, 0)))
    args.append(lam_vecs)
    return pl.pallas_call(
        functools.partial(_diff_kernel, geom, _lambda_init(li)),
        grid_spec=pltpu.PrefetchScalarGridSpec(
            num_scalar_prefetch=1, grid=(b, n_qb, geom.n_chunk), in_specs=in_specs,
            out_specs=pl.BlockSpec((1, 1, 2, 2 * nq, LANE), lambda i, qq, c, pt: (i, qq, 0, 0, 0)),
            scratch_shapes=[pltpu.VMEM((2, 4 * nq, 1), F32), pltpu.VMEM((2, 4 * nq, 1), F32),
                            pltpu.VMEM((2, 4 * nq, LANE), F32)]),
        out_shape=jax.ShapeDtypeStruct((b, n_qb, 2, 2 * nq, LANE), F32),
        compiler_params=_params(("arbitrary", "arbitrary", "arbitrary")),
        name="diff_sample" if geom.sample else "diff_prompt",
    )(pt, *args)


def _moba_kernel(geom, nb_cand, pt_ref, q_ref, *refs):
    n = geom.n_ref
    kv_refs = refs[:n]
    refs = refs[n:]
    if geom.sample:
        new_ref = refs[0]
        refs = refs[1:]
    o_ref, m_sc, l_sc, km_sc, o_sc = refs
    nq, tc = geom.nq, geom.tc
    nrows = N_HEADS * nq
    nbc = tc // MOBA_BLOCK
    ib, qb, c = pl.program_id(0), pl.program_id(1), pl.program_id(2)
    q_base = geom.q_base(qb)
    qpos = q_base + _row_tok(nrows, nq)
    cur = q_base // MOBA_BLOCK
    lane = lax.broadcasted_iota(jnp.int32, (1, LANE), 1)

    def block(kvt, blk, kpos0, masked):
        t = kvt.shape[1]
        kt, vt = kvt[:LANE], kvt[LANE:]
        s = _dot(q_ref[0, 0], kt)
        if masked:
            mask = (kpos0 + lax.broadcasted_iota(jnp.int32, (1, t), 1)) <= qpos
            s = jnp.where(mask, s, NEG)
        mb = jnp.max(s, axis=-1, keepdims=True)
        p = jnp.exp(s - mb)
        if masked:
            p = jnp.where(mask, p, 0.0)
        here = lane == blk
        m_sc[...] = jnp.where(here, mb, m_sc[...])
        l_sc[...] = jnp.where(here, jnp.sum(p, axis=-1, keepdims=True), l_sc[...])
        km_sc[...] = jnp.where(here, jnp.sum(kt, axis=-1, keepdims=True) * (1.0 / MOBA_BLOCK), km_sc[...])
        o_sc[blk] = _dot_nt(p, vt)

    @pl.when((ib == 0) & (qb == 0) & (c == 0))
    def _():
        m_sc[...] = jnp.full(m_sc.shape, NEG, F32)
        l_sc[...] = jnp.zeros(l_sc.shape, F32)
        km_sc[...] = jnp.zeros(km_sc.shape, F32)
        o_sc[...] = jnp.zeros(o_sc.shape, F32)

    if geom.sample:
        @pl.when(c == 0)
        def _():
            block(new_ref[0], nb_cand, geom.past_len, True)

    @pl.when(geom.chunk_live(qb, c))
    def _():
        kvt = _cat([r[0, 0] if geom.sample else r[0] for r in kv_refs], 1)
        for j in range(nbc):
            blk = c * nbc + j
            sub = kvt[:, j * MOBA_BLOCK:(j + 1) * MOBA_BLOCK]
            if geom.sample:
                block(sub, blk, blk * MOBA_BLOCK, False)
            else:
                @pl.when(blk <= cur)
                def _():
                    block(sub, blk, blk * MOBA_BLOCK, True)

    @pl.when(c == geom.n_chunk - 1)
    def _():
        gate = _dot_xx(q_ref[0, 0], km_sc[...])
        past = lane < cur
        rank = jnp.zeros(gate.shape, jnp.int32)
        for mth in range(nb_cand):
            gm = gate[:, mth:mth + 1]
            ahead = (gm > gate) | ((gm == gate) & (mth < lane))
            if not geom.sample:
                ahead = ahead & (mth < cur)
            rank = rank + ahead.astype(jnp.int32)
        sel = (past & (rank < MOBA_TOPK) & (jnp.abs(gate) < jnp.inf)) | (lane == cur)
        mm = jnp.where(sel, m_sc[...], NEG)
        mtop = jnp.max(mm, axis=-1, keepdims=True)
        w = jnp.where(sel, jnp.exp(mm - mtop), 0.0)
        den = jnp.sum(w * l_sc[...], axis=-1, keepdims=True)
        acc = jnp.zeros((nrows, LANE), F32)
        for blk in range(nb_cand + 1 if geom.sample else nb_cand):
            acc = acc + w[:, blk:blk + 1] * o_sc[blk]
        o_ref[0, 0] = acc / jnp.maximum(den, 1e-30)


def moba_attention(geom, li, pt, q, kvt_src, new):
    b, n_qb, nrows, _ = q.shape
    n, tc, tpr = geom.n_ref, geom.tc, geom.tok_per_ref
    nb_cand = geom.n_chunk * (tc // MOBA_BLOCK)
    assert nb_cand + 1 <= LANE
    in_specs = [pl.BlockSpec((1, 1, nrows, LANE), lambda i, qq, c, pt: (i, qq, 0, 0))]
    args = [q]
    if geom.sample:
        for j in range(n):
            in_specs.append(pl.BlockSpec((1, 1, 2 * LANE, tpr),
                                         functools.partial(lambda i, qq, c, pt, j: (li, pt[i, c * n + j], 0, 0), j=j)))
        args += [kvt_src] * n
        in_specs.append(pl.BlockSpec((1, 2 * LANE, LANE), lambda i, qq, c, pt: (i, 0, 0)))
        args.append(new)
    else:
        in_specs.append(pl.BlockSpec((1, 2 * LANE, tc), lambda i, qq, c, pt: (i, 0, _chunk_idx(geom, qq, c))))
        args.append(kvt_src)
    return pl.pallas_call(
        functools.partial(_moba_kernel, geom, nb_cand),
        grid_spec=pltpu.PrefetchScalarGridSpec(
            num_scalar_prefetch=1, grid=(b, n_qb, geom.n_chunk), in_specs=in_specs,
            out_specs=pl.BlockSpec((1, 1, nrows, LANE), lambda i, qq, c, pt: (i, qq, 0, 0)),
            scratch_shapes=[pltpu.VMEM((nrows, LANE), F32), pltpu.VMEM((nrows, LANE), F32),
                            pltpu.VMEM((LANE, LANE), F32), pltpu.VMEM((nb_cand + 1, nrows, LANE), F32)]),
        out_shape=jax.ShapeDtypeStruct((b, n_qb, nrows, LANE), F32),
        compiler_params=_params(("arbitrary", "arbitrary", "arbitrary")),
        name="moba_sample" if geom.sample else "moba_prompt",
    )(pt, *args)


def _gelu_tanh(x):
    return 0.5 * x * (1.0 + jnp.tanh(math.sqrt(2.0 / math.pi) * (x + 0.044715 * x * x * x)))


def _compress_kernel(geom, nseg, pt_ref, *refs):
    n = geom.n_ref
    kv_refs = refs[:n]
    (ptop_ref, pbot_ref, wtop_ref, wbot_ref, b1_ref, w2_ref, kg_ref, c_ref, sa_ref, sb_ref,
     o_ref, x_sc, a_sc, b_sc) = refs[n:]
    tc, tpr = geom.tc, geom.tok_per_ref
    segc = tc // NSA_CMP_STRIDE
    c = pl.program_id(1)
    for j, r in enumerate(kv_refs):
        if geom.sample:
            x_sc[j * tpr:(j + 1) * tpr, :] = r[0, 0].T
        else:
            x_sc[j * tpr:(j + 1) * tpr, :] = r[0]
    u = jnp.concatenate([x_sc[pl.ds(r, segc, stride=NSA_CMP_STRIDE), :] for r in range(NSA_CMP_STRIDE)], axis=1)
    row0 = pl.multiple_of(c * segc, 8)
    a_sc[pl.ds(row0, segc), :] = _dot(u + ptop_ref[...], wtop_ref[...])
    b_sc[pl.ds(row0, segc), :] = _dot(u + pbot_ref[...], wbot_ref[...])

    @pl.when(c == geom.n_chunk - 1)
    def _():
        hid = a_sc[...] + pltpu.roll(b_sc[...], nseg - 1, 0) + b1_ref[...]
        kv = _dot(_gelu_tanh(hid), w2_ref[...])
        lane = lax.broadcasted_iota(jnp.int32, (1, LANE), 1)
        ri = lax.broadcasted_iota(jnp.int32, (LANE, LANE), 0) // HEAD_DIM
        ci = lax.broadcasted_iota(jnp.int32, (LANE, LANE), 1) // HEAD_DIM
        ss = _dot_x(kv * kv, (ri == ci).astype(BF16))
        y = kv * lax.rsqrt(ss * (1.0 / HEAD_DIM) + EPS) * kg_ref[...]
        y = _rope_apply(y, c_ref[...], sa_ref[...], sb_ref[...], ROT_DIM // 2)
        o_ref[0] = jnp.where(lane < HEAD_DIM, y, kv)


def nsa_compress(geom, li, pt, batch, kv_src, nseg, lw, tabs):
    n, tc, tpr = geom.n_ref, geom.tc, geom.tok_per_ref
    if geom.sample:
        in_specs = [pl.BlockSpec((1, 1, LANE, tpr),
                                 functools.partial(lambda i, c, pt, j: (li, pt[i, c * n + j], 0, 0), j=j))
                    for j in range(n)]
    else:
        in_specs = [pl.BlockSpec((1, tc, LANE), lambda i, c, pt: (i, c, 0))]
    consts = (lw["cmp_ptop"], lw["cmp_pbot"], lw["cmp_wtop"], lw["cmp_wbot"], lw["cmp_b1"], lw["cmp_w2"],
              lw["cmp_kg"]) + tuple(tabs)
    in_specs += [pl.BlockSpec(a.shape, lambda i, c, pt: (0, 0)) for a in consts]
    return pl.pallas_call(
        functools.partial(_compress_kernel, geom, nseg),
        grid_spec=pltpu.PrefetchScalarGridSpec(
            num_scalar_prefetch=1, grid=(batch, geom.n_chunk), in_specs=in_specs,
            out_specs=pl.BlockSpec((1, nseg, LANE), lambda i, c, pt: (i, 0, 0)),
            scratch_shapes=[pltpu.VMEM((tc, LANE), F32), pltpu.VMEM((nseg, 2 * LANE), F32),
                            pltpu.VMEM((nseg, 2 * LANE), F32)]),
        out_shape=jax.ShapeDtypeStruct((batch, nseg, LANE), F32),
        compiler_params=_params(("arbitrary", "arbitrary")),
        name="nsa_compress_sample" if geom.sample else "nsa_compress_prompt",
    )(pt, *([kv_src] * n), *consts)


def _win_kernel(geom, pt_ref, q_ref, *refs):
    o_ref = refs[-1]
    kv_refs = refs[:-1]
    nq = geom.nq
    nrows = N_HEADS * nq
    qb = pl.program_id(1)
    q_base = geom.q_base(qb)
    qpos = q_base + _row_tok(nrows, nq)
    if geom.sample:
        kvt = jnp.concatenate([kv_refs[0][0, 0], kv_refs[1][0]], axis=1)
        kpos0 = geom.past_len - NSA_WINDOW
    else:
        kvt = jnp.concatenate([r[0] for r in kv_refs], axis=1)
        kpos0 = q_base - NSA_WINDOW
    t = kvt.shape[1]
    kpos = kpos0 + lax.broadcasted_iota(jnp.int32, (1, t), 1)
    d = qpos - kpos
    mask = (d >= 0) & (d < NSA_WINDOW) & (kpos >= 0)
    s = jnp.where(mask, _dot(q_ref[0, 0], kvt), NEG)
    p = jnp.where(mask, jnp.exp(s - jnp.max(s, axis=-1, keepdims=True)), 0.0)
    o = _dot_nt(p, kvt)
    o_ref[0, 0] = o / jnp.maximum(jnp.sum(p, axis=-1, keepdims=True), 1e-30)


def nsa_window(geom, li, pt, q, wint_src, new):
    b, n_qb, nrows, _ = q.shape
    in_specs = [pl.BlockSpec((1, 1, nrows, LANE), lambda i, qq, pt: (i, qq, 0, 0))]
    if geom.sample:
        in_specs += [pl.BlockSpec((1, 1, LANE, NSA_WINDOW), lambda i, qq, pt: (li, i, 0, 0)),
                     pl.BlockSpec((1, LANE, LANE), lambda i, qq, pt: (i, 0, 0))]
        args = [wint_src, new]
    else:
        nback = NSA_WINDOW // geom.nq
        in_specs += [pl.BlockSpec((1, LANE, geom.nq),
                                  functools.partial(lambda i, qq, pt, j: (i, 0, jnp.maximum(qq - nback + j, 0)), j=j))
                     for j in range(nback + 1)]
        args = [wint_src] * (nback + 1)
    return pl.pallas_call(
        functools.partial(_win_kernel, geom),
        grid_spec=pltpu.PrefetchScalarGridSpec(
            num_scalar_prefetch=1, grid=(b, n_qb), in_specs=in_specs,
            out_specs=pl.BlockSpec((1, 1, nrows, LANE), lambda i, qq, pt: (i, qq, 0, 0))),
        out_shape=jax.ShapeDtypeStruct((b, n_qb, nrows, LANE), F32),
        compiler_params=_params(("arbitrary", "arbitrary")),
        name="nsa_window_sample" if geom.sample else "nsa_window_prompt",
    )(pt, q, *args)


def _nsa_kernel(geom, nc, nsb, pt_ref, q_ref, kvc_ref, gate_ref, owin_ref, *refs):
    n = geom.n_ref
    kv_refs = refs[:n]
    refs = refs[n:]
    if geom.sample:
        new_ref = refs[0]
        refs = refs[1:]
    o_ref, m_sc, l_sc, acc_sc, ocmp_sc, sel_sc = refs
    nq, tc = geom.nq, geom.tc
    nrows = N_HEADS * nq
    nsbp = sel_sc.shape[1]
    qb, c = pl.program_id(1), pl.program_id(2)
    q_base = geom.q_base(qb)
    qtok = q_base + lax.broadcasted_iota(jnp.int32, (nq, 1), 0)

    def select_and_compress():
        kvc = kvc_ref[0]
        ncp = kvc.shape[0]
        ci = lax.broadcasted_iota(jnp.int32, (1, ncp), 1)
        cend = ci * NSA_CMP_STRIDE + (NSA_CMP_LEN - 1)
        mask = ((cend <= qtok) & (ci < nc))[None]
        s = _dot_nt(q_ref[0, 0], kvc).reshape(N_HEADS, nq, ncp)
        s = jnp.where(mask, s, NEG)
        e = jnp.where(mask, jnp.exp(s - jnp.max(s, axis=-1, keepdims=True)), 0.0)
        p = e / jnp.maximum(jnp.sum(e, axis=-1, keepdims=True), 1e-30)
        ocmp_sc[...] = _dot(p.reshape(nrows, ncp), kvc)
        cs = lax.broadcasted_iota(jnp.int32, (ncp, 1), 0) * NSA_CMP_STRIDE
        bs = lax.broadcasted_iota(jnp.int32, (1, nsbp), 1) * NSA_SEL_BLOCK
        covers = ((cs < bs + NSA_SEL_BLOCK) & (cs + NSA_CMP_LEN > bs)).astype(BF16)
        imp = _dot_x(jnp.sum(p, axis=0), covers)
        blk = lax.broadcasted_iota(jnp.int32, (1, nsbp), 1)
        curb = qtok // NSA_SEL_BLOCK
        forced = (blk == curb) | (blk == 0)
        score = jnp.where(forced, SEL_FORCE, jnp.where(blk <= curb, imp, -jnp.inf))
        rank = jnp.zeros(score.shape, jnp.int32)
        for mth in range(nsb):
            sm = score[:, mth:mth + 1]
            rank = rank + ((sm > score) | ((sm == score) & (mth < blk))).astype(jnp.int32)
        sel_sc[...] = ((rank < NSA_TOPN) & (score > -jnp.inf)).astype(F32)

    def process(kvt, kpos0):
        t = kvt.shape[1]
        kpos = kpos0 + lax.broadcasted_iota(jnp.int32, (1, t), 1)
        expand = (lax.broadcasted_iota(jnp.int32, (nsbp, 1), 0) == kpos // NSA_SEL_BLOCK).astype(BF16)
        picked = jnp.dot(sel_sc[...].astype(BF16), expand, preferred_element_type=F32) > 0.5
        mask = picked & (kpos <= qtok)
        s = _dot(q_ref[0, 0], kvt)
        _flash_update(s, mask, lambda p: _dot_nt(p, kvt), m_sc, l_sc, acc_sc)

    @pl.when(c == 0)
    def _():
        _flash_init(m_sc, l_sc, acc_sc)
        select_and_compress()
        if geom.sample:
            process(new_ref[0], geom.past_len)

    @pl.when(geom.chunk_live(qb, c))
    def _():
        process(_cat([r[0, 0] if geom.sample else r[0] for r in kv_refs], 1), c * tc)

    @pl.when(c == geom.n_chunk - 1)
    def _():
        g = gate_ref[0, 0]
        o_ref[0, 0] = (g[:, 0:1] * ocmp_sc[...] + g[:, 1:2] * _flash_out(l_sc, acc_sc)
                       + g[:, 2:3] * owin_ref[0, 0])


def nsa_attention(geom, li, pt, q, kvc, gates, o_win, kvt_src, new, nc, nsb):
    b, n_qb, nrows, _ = q.shape
    n, tc, tpr = geom.n_ref, geom.tc, geom.tok_per_ref
    nsbp = -(-nsb // LANE) * LANE
    rspec = pl.BlockSpec((1, 1, nrows, LANE), lambda i, qq, c, pt: (i, qq, 0, 0))
    in_specs = [rspec, pl.BlockSpec((1,) + kvc.shape[1:], lambda i, qq, c, pt: (i, 0, 0)), rspec, rspec]
    args = [q, kvc, gates, o_win]
    if geom.sample:
        for j in range(n):
            in_specs.append(pl.BlockSpec((1, 1, LANE, tpr),
                                         functools.partial(lambda i, qq, c, pt, j: (li, pt[i, c * n + j], 1, 0), j=j)))
        args += [kvt_src] * n
        in_specs.append(pl.BlockSpec((1, LANE, LANE), lambda i, qq, c, pt: (i, 0, 0)))
        args.append(new)
    else:
        in_specs.append(pl.BlockSpec((1, LANE, tc), lambda i, qq, c, pt: (i, 1, _chunk_idx(geom, qq, c))))
        args.append(kvt_src)
    return pl.pallas_call(
        functools.partial(_nsa_kernel, geom, nc, nsb),
        grid_spec=pltpu.PrefetchScalarGridSpec(
            num_scalar_prefetch=1, grid=(b, n_qb, geom.n_chunk), in_specs=in_specs,
            out_specs=rspec,
            scratch_shapes=[pltpu.VMEM((nrows, 1), F32), pltpu.VMEM((nrows, 1), F32),
                            pltpu.VMEM((nrows, LANE), F32), pltpu.VMEM((nrows, LANE), F32),
                            pltpu.VMEM((geom.nq, nsbp), F32)]),
        out_shape=jax.ShapeDtypeStruct((b, n_qb, nrows, LANE), F32),
        compiler_params=_params(("arbitrary", "arbitrary", "arbitrary")),
        name="nsa_sample" if geom.sample else "nsa_prompt",
    )(pt, *args)


def _to_rows(x, nq):
    b, t, h, d = x.shape
    return x.reshape(b, t // nq, nq, h, d).transpose(0, 1, 3, 2, 4).reshape(b, t // nq, h * nq, d)


def _from_rows(x, nq, h):
    b, n_qb, _, d = x.shape
    return x.reshape(b, n_qb, h, nq, d).transpose(0, 1, 3, 2, 4).reshape(b * n_qb * nq, h * d)


def _pad_group_lanes(q, groups):
    h = q.shape[2]
    half = jnp.arange(h) // (h // groups)
    z = jnp.zeros_like(q)
    lo = jnp.where((half == 0)[None, None, :, None], q, z)
    hi = jnp.where((half == 1)[None, None, :, None], q, z)
    return jnp.concatenate([lo, hi], axis=-1)


def _rope_tables(pos, half, theta, period, n_rep):
    inv = theta ** (-jnp.arange(half, dtype=F32) / half)
    ang = pos.astype(F32)[:, None] * inv[None, :]
    cos, sin = jnp.cos(ang), jnp.sin(ang)
    t = pos.shape[0]
    ones = jnp.ones((t, period - 2 * half), F32)
    zeros = jnp.zeros((t, period - 2 * half), F32)
    zh = jnp.zeros((t, half), F32)
    c = jnp.concatenate([cos, cos, ones], axis=1)
    sa = jnp.concatenate([-sin, zh, zeros], axis=1)
    sb = jnp.concatenate([zh, sin, zeros], axis=1)
    return tuple(jnp.tile(a, (1, n_rep)) for a in (c, sa, sb))


def _layer_weights(li, p):
    w_in = p["w_in"][li]
    d = w_in.shape[0]
    wa = jnp.concatenate([w_in[:, 0:512], w_in[:, 544:3232], w_in[:, 512:544], w_in[:, 3232:3256],
                          jnp.zeros((d, LANE - 56), F32)], axis=1).astype(BF16)
    lw = {"wa": wa, "wg": w_in[:, 3256:].astype(BF16)}
    lw["cq_g"] = p["mla_cq_g"][li].reshape(1, -1)
    lw["ckv_g"] = p["mla_ckv_g"][li].reshape(1, -1)
    wuq = p["mla_w_uq"][li]
    lw["wuq"] = jnp.concatenate([wuq[:, :, :MLA_ROPE].reshape(384, -1),
                                 wuq[:, :, MLA_ROPE:].reshape(384, -1)], axis=1).astype(BF16)
    head = jnp.concatenate([jnp.arange(256) // MLA_ROPE, jnp.arange(512) // MLA_NOPE])
    lw["b768"] = (head[:, None] == head[None, :]).astype(BF16)
    qg = p["mla_qn_g"][li]
    lw["q_g768"] = jnp.concatenate([jnp.tile(qg[:MLA_ROPE], 8), jnp.tile(qg[MLA_ROPE:], 8)]).reshape(1, -1)
    kg = p["mla_kn_g"][li]
    wuk = p["mla_w_uk"][li]
    wqa = jnp.zeros((8, MLA_NOPE, 8, LANE), F32)
    wqa = wqa.at[jnp.arange(8), :, jnp.arange(8), :].set(
        jnp.transpose(wuk, (1, 2, 0)) * kg[MLA_ROPE:][None, :, None])
    lw["wqa"] = wqa.reshape(8 * MLA_NOPE, 8 * LANE).astype(BF16)
    lw["wukt"] = jnp.transpose(wuk, (1, 2, 0)).reshape(8 * MLA_NOPE, LANE).astype(BF16)
    lw["k_gpe"] = kg[:MLA_ROPE].reshape(MLA_ROPE, 1)
    lw["wuv"] = jnp.transpose(p["mla_w_uv"][li], (1, 0, 2)).astype(BF16)
    gains = {"q_moba": p["moba_qn_g"][li], "moba_kv": p["moba_kn_g"][li], "q_diff": p["diff_qn_g"][li],
             "diff_kv": p["diff_kn_g"][li], "q_nsa": p["nsa_qn_g"][li]}
    one = jnp.ones((HEAD_DIM,), F32)
    sg = []
    for name, width, en in SLAB_GROUPS:
        for k, e in enumerate(en):
            if name == "nsa_kv":
                sg.append(p["nsa_kn_g"][li, 1] if e else one)
            elif name == "nsa_win":
                sg.append(p["nsa_kn_g"][li, 2] if e else one)
            else:
                sg.append(gains[name] if e else one)
    lw["slab_g"] = jnp.concatenate(sg).reshape(1, -1)
    w1 = p["nsa_cmp_w1"][li].reshape(2, NSA_CMP_LEN, HEAD_DIM, -1)
    hid = w1.shape[-1]
    wkv = jnp.zeros((NSA_CMP_LEN, 2, HEAD_DIM, 2, hid), F32)
    wkv = wkv.at[:, 0, :, 0, :].set(w1[0]).at[:, 1, :, 1, :].set(w1[1])
    wkv = wkv.reshape(NSA_CMP_LEN * 2 * HEAD_DIM, 2 * hid)
    half = NSA_CMP_STRIDE * 2 * HEAD_DIM
    lw["cmp_wtop"] = wkv[:half].astype(BF16)
    lw["cmp_wbot"] = wkv[half:].astype(BF16)
    pos = jnp.transpose(p["nsa_cmp_pos"][li], (1, 0, 2)).reshape(1, NSA_CMP_LEN * 2 * HEAD_DIM)
    lw["cmp_ptop"] = pos[:, :half]
    lw["cmp_pbot"] = pos[:, half:]
    lw["cmp_b1"] = p["nsa_cmp_b1"][li].reshape(1, 2 * hid)
    w2 = p["nsa_cmp_w2"][li]
    w2b = jnp.zeros((2, hid, 2, HEAD_DIM), F32).at[0, :, 0, :].set(w2[0]).at[1, :, 1, :].set(w2[1])
    lw["cmp_w2"] = w2b.reshape(2 * hid, 2 * HEAD_DIM).astype(BF16)
    lw["cmp_kg"] = jnp.concatenate([p["nsa_kn_g"][li, 0], one]).reshape(1, LANE)
    lw["w_branch"] = p["w_branch"][li].astype(BF16)
    lw["w_out"] = p["w_out"][li].astype(BF16)
    lw["w_gu"] = p["ffn_w_gu"][li].astype(BF16)
    lw["w_down"] = p["ffn_w_down"][li].astype(BF16)
    return lw


def _attention_group(geom, geom_whole, li, pt, r, b, t, srcs, tabs, lw, lam_vecs):
    nq = geom.nq
    n_qb = t // nq
    shp = lambda a, h, d: a.reshape(b, t, h, d)
    qa = _to_rows(shp(r["qa"], 8, LANE), nq)
    qpe = _to_rows(jnp.pad(shp(r["qpe"], 8, MLA_ROPE), ((0, 0),) * 3 + ((0, LANE - MLA_ROPE),)), nq)
    o_mla = mla_attention(geom, li, pt, qa, qpe, srcs["ckv"], srcs["kpet"], srcs.get("mla_new"),
                          tabs["mla"], lw)
    o_mla = _from_rows(o_mla, nq, 8)
    qm = _to_rows(_pad_group_lanes(shp(r["q_moba"], 8, HEAD_DIM), 2), nq)
    o_moba = moba_attention(geom_whole, li, pt, qm, srcs["moba"], srcs.get("moba_new"))
    o_moba = _from_rows(o_moba, nq, 8).reshape(b * t, 8, 2, HEAD_DIM)
    o_moba = jnp.concatenate([o_moba[:, :4, 0], o_moba[:, 4:, 1]], axis=1).reshape(b * t, 512)
    qd = shp(r["q_diff"], 8, HEAD_DIM).reshape(b, t, 2, 2, 2, HEAD_DIM)
    qd = jnp.transpose(qd, (0, 1, 2, 4, 3, 5))
    z = jnp.zeros_like(qd[:, :, :, 0])
    qd = jnp.stack([jnp.concatenate([qd[:, :, :, 0], z], -1), jnp.concatenate([z, qd[:, :, :, 1]], -1)], axis=3)
    qd = qd.reshape(b, n_qb, nq, 2, 4, LANE).transpose(0, 1, 3, 4, 2, 5).reshape(b, n_qb, 2, 4 * nq, LANE)
    o_diff = diff_attention(geom, li, pt, qd, srcs["diff"], srcs.get("diff_new"), lam_vecs)
    o_diff = o_diff.reshape(b, n_qb, 2, 2, nq, LANE).transpose(0, 1, 4, 2, 3, 5).reshape(b * t, 512)
    qn = shp(r["q_nsa"], 8, HEAD_DIM)
    qn = _to_rows(jnp.concatenate([qn, jnp.zeros_like(qn)], axis=-1), nq)
    gates = shp(r["last"][:, MLA_ROPE:MLA_ROPE + 24], 8, 3)
    gates = _to_rows(jnp.pad(gates, ((0, 0),) * 3 + ((0, LANE - 3),)), nq)
    o_win = nsa_window(geom, li, pt, qn, srcs["win"], srcs.get("win_new"))
    kvc = nsa_compress(geom_whole, li, pt, b, srcs["cmp"], srcs["nseg"], lw, tabs["cmp"])
    o_nsa = nsa_attention(geom, li, pt, qn, kvc, gates, o_win, srcs["nsa"], srcs.get("nsa_new"),
                          srcs["nseg"] - 1, srcs["nsb"])
    o_nsa = _from_rows(o_nsa[..., HEAD_DIM:], nq, 8)
    return o_mla, o_moba, o_diff, o_nsa


def kernel(x_prompt, x_sample, cache_mla_ckv, cache_mla_kpe, cache_moba_kv, cache_diff_kv, cache_nsa_kv, state_nsa_win, page_table, ln_attn_g, w_in, mla_cq_g, mla_ckv_g, mla_w_uq, mla_qn_g, mla_w_uk, mla_w_uv, mla_kn_g, moba_qn_g, moba_kn_g, diff_qn_g, diff_kn_g, diff_lambda, diff_subln_g, nsa_qn_g, nsa_kn_g, nsa_cmp_pos, nsa_cmp_w1, nsa_cmp_b1, nsa_cmp_w2, w_branch, w_out, ln_ffn_g, ffn_w_gu, ffn_w_down):
    p = dict(w_in=w_in, mla_cq_g=mla_cq_g, mla_ckv_g=mla_ckv_g, mla_w_uq=mla_w_uq, mla_qn_g=mla_qn_g,
             mla_w_uk=mla_w_uk, mla_w_uv=mla_w_uv, mla_kn_g=mla_kn_g, moba_qn_g=moba_qn_g,
             moba_kn_g=moba_kn_g, diff_qn_g=diff_qn_g, diff_kn_g=diff_kn_g, nsa_qn_g=nsa_qn_g,
             nsa_kn_g=nsa_kn_g, nsa_cmp_pos=nsa_cmp_pos, nsa_cmp_w1=nsa_cmp_w1, nsa_cmp_b1=nsa_cmp_b1,
             nsa_cmp_w2=nsa_cmp_w2, w_branch=w_branch, w_out=w_out, ffn_w_gu=ffn_w_gu, ffn_w_down=ffn_w_down)
    bp, tp, d = x_prompt.shape
    bs, ts, _ = x_sample.shape
    depth, n_pool, page = cache_mla_ckv.shape[:3]
    n_pages = page_table.shape[1]
    past = n_pages * page
    mp, ms = bp * tp, bs * ts
    assert page == LANE and past % MOBA_BLOCK == 0 and state_nsa_win.shape[2] == NSA_WINDOW
    assert tp % MOBA_BLOCK == 0 and tp >= NSA_WINDOW and ts <= 8
    nq_p = min(Q_BLOCK, tp)
    n_ref = min(PAGES_PER_STEP, n_pages)
    assert n_pages % n_ref == 0 and (n_ref * page) % MOBA_BLOCK == 0
    tc_p = min(512, tp)
    geom_p = _Geom(False, nq_p, tp // nq_p, tc_p, 1, tp // tc_p, 0)
    geom_s = _Geom(True, ts, 1, page, n_ref, n_pages // n_ref, past)
    geom_pm = _Geom(False, nq_p, tp // nq_p, tp, 1, 1, 0)
    pt_dummy = jnp.zeros((1, 1), jnp.int32)

    fm = lambda a: jnp.transpose(a, (0, 1, 3, 4, 5, 2)).reshape(a.shape[0], a.shape[1], -1, a.shape[2])
    moba_pool = fm(cache_moba_kv)
    nsa_pool = fm(cache_nsa_kv)
    win_state = fm(state_nsa_win)
    kpe_pool = jnp.transpose(cache_mla_kpe, (0, 1, 3, 2))
    diff_pool = cache_diff_kv.reshape(depth, n_pool, 4 * page, LANE)

    pos_p = jnp.arange(tp)
    pos_s = past + jnp.arange(ts)
    pos_all = jnp.concatenate([jnp.tile(pos_p, bp), jnp.tile(pos_s, bs)])
    tab64 = _rope_tables(pos_all, ROT_DIM // 2, ROPE_THETA, HEAD_DIM, 2)
    tabm = _rope_tables(pos_all, MLA_ROPE // 2, MLA_THETA, MLA_ROPE, 4)
    row_tabs = tab64 + tabm

    def key_tabs(npos):
        inv = MLA_THETA ** (-jnp.arange(MLA_ROPE // 2, dtype=F32) / (MLA_ROPE // 2))
        ang = inv[:, None] * jnp.arange(npos, dtype=F32)[None, :]
        return jnp.cos(ang), jnp.sin(ang)

    cos_p, sin_p = key_tabs(tp)
    cos_s, sin_s = key_tabs(past + LANE)
    mla_tabs_p = (cos_p, sin_p, None, None)
    mla_tabs_s = (cos_s[:, :past], sin_s[:, :past], cos_s[:, past:], sin_s[:, past:])

    def cmp_tabs(nseg):
        cend = jnp.arange(nseg) * NSA_CMP_STRIDE + NSA_CMP_LEN - 1
        c, sa, sb = _rope_tables(cend, ROT_DIM // 2, ROPE_THETA, HEAD_DIM, 1)
        one = jnp.ones((nseg, HEAD_DIM), F32)
        zero = jnp.zeros((nseg, HEAD_DIM), F32)
        return (jnp.concatenate([c, one], 1), jnp.concatenate([sa, zero], 1), jnp.concatenate([sb, zero], 1))

    nseg_p, nseg_s = tp // NSA_CMP_STRIDE, past // NSA_CMP_STRIDE
    cmp_tabs_p, cmp_tabs_s = cmp_tabs(nseg_p), cmp_tabs(nseg_s)

    x = jnp.concatenate([x_prompt.reshape(mp, d), x_sample.reshape(ms, d)], axis=0)
    news = {k: [] for k in ("ckv", "kpe", "moba", "diff", "nsa", "win")}
    for li in range(depth):
        lw = _layer_weights(li, p)
        z = mm_norm(x, ln_attn_g[li], lw["wa"])
        r = token_rows(z, lw, row_tabs)
        rp = {k: v[:mp] for k, v in r.items()}
        rs = {k: v[mp:] for k, v in r.items()}
        for k, src in (("ckv", "ckv"), ("moba", "moba_kv"), ("diff", "diff_kv"), ("nsa", "nsa_kv"),
                       ("win", "nsa_win")):
            news[k].append((rp[src], rs[src]))
        news["kpe"].append((rp["last"][:, :MLA_ROPE], rs["last"][:, :MLA_ROPE]))

        tmaj = lambda a: a.reshape(bp, tp, -1)
        fmaj = lambda a: jnp.swapaxes(tmaj(a), 1, 2)
        srcs_p = {"ckv": tmaj(rp["ckv"]), "kpet": fmaj(rp["last"][:, :MLA_ROPE]), "moba": fmaj(rp["moba_kv"]),
                  "diff": rp["diff_kv"].reshape(bp, tp * 4, LANE), "win": fmaj(rp["nsa_win"]),
                  "cmp": tmaj(rp["nsa_kv"]), "nsa": fmaj(rp["nsa_kv"]), "nseg": nseg_p,
                  "nsb": -(-tp // NSA_SEL_BLOCK)}
        tabs_p = {"mla": mla_tabs_p, "cmp": cmp_tabs_p}
        op = _attention_group(geom_p, geom_pm, li, pt_dummy, rp, bp, tp, srcs_p, tabs_p, lw, diff_lambda[li])

        def new_page(a, feature_major):
            a = jnp.pad(a.reshape(bs, ts, -1), ((0, 0), (0, LANE - ts), (0, 0)))
            return jnp.swapaxes(a, 1, 2) if feature_major else a

        srcs_s = {"ckv": cache_mla_ckv, "kpet": kpe_pool, "moba": moba_pool, "diff": diff_pool,
                  "win": win_state, "cmp": nsa_pool, "nsa": nsa_pool, "nseg": nseg_s,
                  "nsb": -(-(past + ts) // NSA_SEL_BLOCK),
                  "mla_new": (new_page(rs["ckv"], False), new_page(rs["last"][:, :MLA_ROPE], True)),
                  "moba_new": new_page(rs["moba_kv"], True),
                  "diff_new": new_page(rs["diff_kv"], False).reshape(bs, 4 * LANE, LANE),
                  "win_new": new_page(rs["nsa_win"], True),
                  "nsa_new": new_page(rs["nsa_kv"][:, LANE:], True)}
        tabs_s = {"mla": mla_tabs_s, "cmp": cmp_tabs_s}
        osm = _attention_group(geom_s, geom_s, li, page_table, rs, bs, ts, srcs_s, tabs_s, lw, diff_lambda[li])

        outs = [jnp.concatenate([a, b_], axis=0) for a, b_ in zip(op, osm)]
        y = merge_branches(x, ln_attn_g[li], outs, diff_subln_g[li], 1.0 - _lambda_init(li), lw["wg"],
                           lw["w_branch"])
        x = mm_res(y, lw["w_out"], x)
        act = mm_norm_swiglu(x, ln_ffn_g[li], lw["w_gu"])
        x = mm_res(act, lw["w_down"], x)

    def stack(k, tail):
        ps = jnp.stack([a.reshape((bp, tp) + tail) for a, _ in news[k]])
        ss = jnp.stack([b_.reshape((bs, ts) + tail) for _, b_ in news[k]])
        return ps, ss

    ckv_p, ckv_s = stack("ckv", (LANE,))
    kpe_p, kpe_s = stack("kpe", (MLA_ROPE,))
    moba_p, moba_s = stack("moba", (2, 2, HEAD_DIM))
    diff_p, diff_s = stack("diff", (2, 2, 2 * HEAD_DIM))
    nsa_p, nsa_s = stack("nsa", (4, 1, HEAD_DIM))
    win_p, win_s = stack("win", (2, 1, HEAD_DIM))
    win_p = win_p[:, :, -NSA_WINDOW:]
    win_s = jnp.concatenate([state_nsa_win, win_s], axis=2)[:, :, -NSA_WINDOW:]
    return (x[:mp].reshape(bp, tp, d), x[mp:].reshape(bs, ts, d),
            ckv_p, ckv_s, kpe_p, kpe_s, moba_p, moba_s, diff_p, diff_s, nsa_p, nsa_s, win_p, win_s)
```

```python
import functools
import math

import jax
import jax.numpy as jnp
from jax import lax
from jax.experimental import pallas as pl
from jax.experimental.pallas import tpu as pltpu

F32 = jnp.float32
BF16 = jnp.bfloat16

EPS = 1e-6
NEG = -1e30
FAR = 1 << 28
LANE = 128
VMEM_LIMIT = 56 * 1024 * 1024

HEAD_DIM = 64
ROT_DIM = 16
ROPE_THETA = 500000.0
MLA_HEADS = 8
MLA_NOPE = 64
MLA_ROPE = 32
MLA_QK = MLA_NOPE + MLA_ROPE
MLA_THETA = 10000.0
MOBA_BLOCK = 256
MOBA_TOPK = 3
NSA_CMP_LEN = 32
NSA_CMP_STRIDE = 16
NSA_SEL_BLOCK = 64
NSA_TOPN = 16
NSA_WINDOW = 512
SEL_FORCE = 1e30
N_HEADS = 8
MIX_W = N_HEADS * HEAD_DIM
Q_BLOCK = 128
PAGES_PER_STEP = 32


def _lambda_init(li):
    return 0.8 - 0.6 * math.exp(-0.3 * li)


def _tile(n, target, mult):
    best = None
    d = mult
    while d <= min(n, target):
        if n % d == 0:
            best = d
        d += mult
    return best if best is not None else n


def _params(sem):
    return pltpu.CompilerParams(dimension_semantics=sem, vmem_limit_bytes=VMEM_LIMIT)


def _dot(a, b):
    return jnp.dot(a.astype(BF16), b.astype(BF16), preferred_element_type=F32)


def _dot_nt(a, b):
    return lax.dot_general(a.astype(BF16), b.astype(BF16), (((1,), (1,)), ((), ())),
                           preferred_element_type=F32)


def _split(a):
    hi = a.astype(BF16)
    lo = (a - hi.astype(F32)).astype(BF16)
    return hi, lo


def _dot_x(a, b_exact):
    hi, lo = _split(a)
    b = b_exact.astype(BF16)
    return (jnp.dot(hi, b, preferred_element_type=F32)
            + jnp.dot(lo, b, preferred_element_type=F32))


def _dot_xx(a, b):
    ah, al = _split(a)
    bh, bl = _split(b)
    return (jnp.dot(ah, bh, preferred_element_type=F32)
            + jnp.dot(ah, bl, preferred_element_type=F32)
            + jnp.dot(al, bh, preferred_element_type=F32))


def _rms(x, g):
    r = lax.rsqrt(jnp.mean(x * x, axis=-1, keepdims=True) + EPS)
    return x * r * g


def _mm_norm_kernel(x_ref, g_ref, w_ref, o_ref, h_sc):
    @pl.when(pl.program_id(1) == 0)
    def _():
        h_sc[...] = _rms(x_ref[...], g_ref[...]).astype(BF16)

    o_ref[...] = jnp.dot(h_sc[...], w_ref[...], preferred_element_type=F32)


def mm_norm(x, g, w):
    m, k = x.shape
    n = w.shape[1]
    tm = _tile(m, 512, 8)
    tn = _tile(n, 2048, LANE)
    return pl.pallas_call(
        _mm_norm_kernel,
        grid=(m // tm, n // tn),
        in_specs=[pl.BlockSpec((tm, k), lambda i, j: (i, 0)),
                  pl.BlockSpec((1, k), lambda i, j: (0, 0)),
                  pl.BlockSpec((k, tn), lambda i, j: (0, j))],
        out_specs=pl.BlockSpec((tm, tn), lambda i, j: (i, j)),
        out_shape=jax.ShapeDtypeStruct((m, n), F32),
        scratch_shapes=[pltpu.VMEM((tm, k), BF16)],
        compiler_params=_params(("parallel", "arbitrary")),
        name="mm_norm",
    )(x, g.reshape(1, k), w)


def _swiglu_kernel(x_ref, g_ref, wg_ref, wu_ref, o_ref, h_sc):
    @pl.when(pl.program_id(1) == 0)
    def _():
        h_sc[...] = _rms(x_ref[...], g_ref[...]).astype(BF16)

    h = h_sc[...]
    gate = jnp.dot(h, wg_ref[...], preferred_element_type=F32)
    up = jnp.dot(h, wu_ref[...], preferred_element_type=F32)
    o_ref[...] = (gate * jax.nn.sigmoid(gate) * up).astype(o_ref.dtype)


def mm_norm_swiglu(x, g, w_gu):
    m, k = x.shape
    hid = w_gu.shape[1] // 2
    tm = _tile(m, 1024, 8)
    tn = _tile(hid, 512, LANE)
    nj = hid // tn
    return pl.pallas_call(
        _swiglu_kernel,
        grid=(m // tm, nj),
        in_specs=[pl.BlockSpec((tm, k), lambda i, j: (i, 0)),
                  pl.BlockSpec((1, k), lambda i, j: (0, 0)),
                  pl.BlockSpec((k, tn), lambda i, j: (0, j)),
                  pl.BlockSpec((k, tn), lambda i, j: (0, j + nj))],
        out_specs=pl.BlockSpec((tm, tn), lambda i, j: (i, j)),
        out_shape=jax.ShapeDtypeStruct((m, hid), BF16),
        scratch_shapes=[pltpu.VMEM((tm, k), BF16)],
        compiler_params=_params(("parallel", "arbitrary")),
        name="ffn_gate_up",
    )(x, g.reshape(1, k), w_gu, w_gu)


def _mm_res_kernel(a_ref, w_ref, r_ref, o_ref):
    o_ref[...] = r_ref[...] + jnp.dot(a_ref[...], w_ref[...], preferred_element_type=F32)


def mm_res(a, w, res):
    m, k = a.shape
    n = w.shape[1]
    tm = _tile(m, 1024, 8)
    tn = _tile(n, 512, LANE)
    return pl.pallas_call(
        _mm_res_kernel,
        grid=(m // tm, n // tn),
        in_specs=[pl.BlockSpec((tm, k), lambda i, j: (i, 0)),
                  pl.BlockSpec((k, tn), lambda i, j: (0, j)),
                  pl.BlockSpec((tm, tn), lambda i, j: (i, j))],
        out_specs=pl.BlockSpec((tm, tn), lambda i, j: (i, j)),
        out_shape=jax.ShapeDtypeStruct((m, n), F32),
        compiler_params=_params(("parallel", "arbitrary")),
        name="mm_res",
    )(a, w, res)


def _mm_kernel(a_ref, w_ref, o_ref):
    o_ref[...] = jnp.dot(a_ref[...].astype(BF16), w_ref[...], preferred_element_type=F32)


def mm(a, w):
    m, k = a.shape
    n = w.shape[1]
    tm = _tile(m, 512, 8)
    return pl.pallas_call(
        _mm_kernel,
        grid=(m // tm,),
        in_specs=[pl.BlockSpec((tm, k), lambda i: (i, 0)), pl.BlockSpec((k, n), lambda i: (0, 0))],
        out_specs=pl.BlockSpec((tm, n), lambda i: (i, 0)),
        out_shape=jax.ShapeDtypeStruct((m, n), F32),
        compiler_params=_params(("parallel",)),
        name="mm",
    )(a, w)


def _merge_kernel(sub_scale, n_prompt_blocks, x_ref, g_ref, *refs):
    op_refs, os_refs = refs[0:4], refs[4:8]
    sg_ref = refs[8]
    wg_refs, wb_refs = refs[9:13], refs[13:17]
    y_ref, h_sc, o_sc = refs[17:]

    @pl.when(pl.program_id(1) == 0)
    def _():
        h_sc[...] = _rms(x_ref[...], g_ref[...]).astype(BF16)
        from_prompt = pl.program_id(0) < n_prompt_blocks
        for b in range(4):
            o = jnp.where(from_prompt, op_refs[b][...], os_refs[b][...])
            if b == 2:
                for c in range(o.shape[1] // LANE):
                    sl = slice(c * LANE, (c + 1) * LANE)
                    o_sc[b, :, sl] = (_rms(o[:, sl], sg_ref[...]) * sub_scale).astype(BF16)
            else:
                o_sc[b] = o.astype(BF16)

    h = h_sc[...]
    acc = None
    for b in range(4):
        gate = jax.nn.sigmoid(jnp.dot(h, wg_refs[b][...], preferred_element_type=F32))
        yb = jnp.dot(o_sc[b], wb_refs[b][0], preferred_element_type=F32)
        acc = gate * yb if acc is None else acc + gate * yb
    y_ref[...] = acc.astype(y_ref.dtype)


def merge_branches(x, g, outs_p, outs_s, subln_g, sub_scale, w_gate, w_branch):
    m, d = x.shape
    mp, ms = outs_p[0].shape[0], outs_s[0].shape[0]
    tm = _tile(math.gcd(mp, ms), 512, 8)
    npb = mp // tm
    tn = _tile(d, 512, LANE)
    nj = d // tn
    p_spec = pl.BlockSpec((tm, MIX_W), lambda i, j: (jnp.minimum(i, npb - 1), 0))
    s_spec = pl.BlockSpec((tm, MIX_W), lambda i, j: (jnp.maximum(i - npb, 0), 0))
    in_specs = [pl.BlockSpec((tm, d), lambda i, j: (i, 0)),
                pl.BlockSpec((1, d), lambda i, j: (0, 0))] + [p_spec] * 4 + [s_spec] * 4
    in_specs.append(pl.BlockSpec((1, LANE), lambda i, j: (0, 0)))
    for b in range(4):
        in_specs.append(pl.BlockSpec((d, tn), functools.partial(lambda i, j, b: (0, b * nj + j), b=b)))
    for b in range(4):
        in_specs.append(pl.BlockSpec((1, MIX_W, tn), functools.partial(lambda i, j, b: (b, 0, j), b=b)))
    return pl.pallas_call(
        functools.partial(_merge_kernel, sub_scale, npb),
        grid=(m // tm, nj),
        in_specs=in_specs,
        out_specs=pl.BlockSpec((tm, tn), lambda i, j: (i, j)),
        out_shape=jax.ShapeDtypeStruct((m, d), BF16),
        scratch_shapes=[pltpu.VMEM((tm, d), BF16), pltpu.VMEM((4, tm, MIX_W), BF16)],
        compiler_params=_params(("parallel", "arbitrary")),
        name="merge_branches",
    )(x, g.reshape(1, d), *outs_p, *outs_s, subln_g.reshape(1, LANE),
      w_gate, w_gate, w_gate, w_gate, w_branch, w_branch, w_branch, w_branch)


Z_CQ, Z_CKV, Z_SLAB, Z_SLAB_W = 0, 384, 512, 2688
Z_LAST = Z_SLAB + Z_SLAB_W
Z_WIDTH = Z_LAST + LANE
GATE_LANE0 = MLA_ROPE
SLAB_GROUPS = (("q_moba", 512, (1,) * 8), ("moba_kv", 256, (1, 1, 0, 0)),
               ("q_diff", 512, (1,) * 8), ("diff_kv", 512, (1, 1, 1, 1, 0, 0, 0, 0)),
               ("q_nsa", 512, (1,) * 8), ("nsa_kv", 256, (0, 0, 1, 0)), ("nsa_win", 128, (1, 0)))
SLAB_ENABLE = sum((g[2] for g in SLAB_GROUPS), ())
ROW_OUTS = (("qa", 1024), ("qpe", 256), ("ckv", 128), ("last", 128), ("q_moba", 512), ("moba_kv", 256),
            ("q_diff", 512), ("diff_kv", 512), ("q_nsa", 512), ("nsa_kv", 256), ("nsa_win", 128))
ROW_OUTS_T = (("last_t", 128), ("moba_kv_t", 256), ("nsa_kv_t", 256), ("nsa_win_t", 128))


def _rope_apply(y, c, s1, s2, half):
    return y * c + pltpu.roll(y, LANE - half, 1) * s1 + pltpu.roll(y, half, 1) * s2


def _rows_kernel(z_ref, cqg_ref, ckvg_ref, wuq_ref, b768_ref, qg_ref, wqa_ref, slabg_ref,
                 c64_ref, s64a_ref, s64b_ref, cm_ref, sma_ref, smb_ref,
                 qa_ref, qpe_ref, ckv_ref, last_ref, qmoba_ref, mkv_ref, qdiff_ref, dkv_ref,
                 qnsa_ref, nkv_ref, nwin_ref, lastt_ref, mkvt_ref, nkvt_ref, nwint_ref):
    mla_scale = MLA_QK ** -0.5
    cq = _rms(z_ref[:, Z_CQ:Z_CQ + 384], cqg_ref[...])
    q = jnp.dot(cq.astype(BF16), wuq_ref[...], preferred_element_type=F32)
    ss = _dot_x(q * q, b768_ref[...])
    q = q * lax.rsqrt(ss * (1.0 / MLA_QK) + EPS) * qg_ref[...]
    for c in range(2):
        sl = slice(c * LANE, (c + 1) * LANE)
        qr = _rope_apply(q[:, sl], cm_ref[...], sma_ref[...], smb_ref[...], MLA_ROPE // 2)
        qpe_ref[:, sl] = qr * mla_scale
    qa_ref[...] = jnp.dot(q[:, 256:].astype(BF16), wqa_ref[...], preferred_element_type=F32) * mla_scale
    ckv_ref[...] = _rms(z_ref[:, Z_CKV:Z_CKV + LANE], ckvg_ref[...])
    lane = lax.broadcasted_iota(jnp.int32, (1, LANE), 1)
    zl = z_ref[:, Z_LAST:Z_LAST + LANE]
    last = jnp.where(lane < MLA_ROPE, zl, jax.nn.sigmoid(zl))
    last_ref[...] = last
    lastt_ref[...] = last.T
    ri = lax.broadcasted_iota(jnp.int32, (LANE, LANE), 0) // HEAD_DIM
    ci = lax.broadcasted_iota(jnp.int32, (LANE, LANE), 1) // HEAD_DIM
    b128 = (ri == ci).astype(BF16)
    refs = {"q_moba": (qmoba_ref, None), "moba_kv": (mkv_ref, mkvt_ref), "q_diff": (qdiff_ref, None),
            "diff_kv": (dkv_ref, None), "q_nsa": (qnsa_ref, None), "nsa_kv": (nkv_ref, nkvt_ref),
            "nsa_win": (nwin_ref, nwint_ref)}
    c = 0
    for name, width, _ in SLAB_GROUPS:
        ref, ref_t = refs[name]
        for k in range(width // LANE):
            x = z_ref[:, Z_SLAB + c * LANE:Z_SLAB + (c + 1) * LANE]
            e0, e1 = SLAB_ENABLE[2 * c], SLAB_ENABLE[2 * c + 1]
            if e0 or e1:
                ss = _dot_x(x * x, b128)
                y = x * lax.rsqrt(ss * (1.0 / HEAD_DIM) + EPS) * slabg_ref[:, c * LANE:(c + 1) * LANE]
                y = _rope_apply(y, c64_ref[...], s64a_ref[...], s64b_ref[...], ROT_DIM // 2)
                if not (e0 and e1):
                    y = jnp.where((lane < HEAD_DIM) if e0 else (lane >= HEAD_DIM), y, x)
            else:
                y = x
            if name.startswith("q_"):
                y = y * (HEAD_DIM ** -0.5)
            ref[:, k * LANE:(k + 1) * LANE] = y
            if ref_t is not None:
                ref_t[k * LANE:(k + 1) * LANE, :] = y.T
            c += 1


def token_rows(z, lw, tabs):
    m = z.shape[0]
    tm = _tile(m, 256, LANE)
    row = lambda w: pl.BlockSpec((tm, w), lambda i: (i, 0))
    col = lambda w: pl.BlockSpec((w, tm), lambda i: (0, i))
    full = lambda a: pl.BlockSpec(a.shape, lambda i: (0,) * a.ndim)
    consts = (lw["cq_g"], lw["ckv_g"], lw["wuq"], lw["b768"], lw["q_g768"], lw["wqa"], lw["slab_g"])
    outs = pl.pallas_call(
        _rows_kernel,
        grid=(m // tm,),
        in_specs=[row(Z_WIDTH)] + [full(a) for a in consts] + [row(LANE)] * 6,
        out_specs=[row(w) for _, w in ROW_OUTS] + [col(w) for _, w in ROW_OUTS_T],
        out_shape=([jax.ShapeDtypeStruct((m, w), F32) for _, w in ROW_OUTS]
                   + [jax.ShapeDtypeStruct((w, m), F32) for _, w in ROW_OUTS_T]),
        compiler_params=_params(("parallel",)),
        name="token_rows",
    )(z, *consts, *tabs)
    return dict(zip([n for n, _ in ROW_OUTS + ROW_OUTS_T], outs))


def _flash_init(m_sc, l_sc, acc_sc):
    m_sc[...] = jnp.full(m_sc.shape, NEG, F32)
    l_sc[...] = jnp.zeros(l_sc.shape, F32)
    acc_sc[...] = jnp.zeros(acc_sc.shape, F32)


def _masked(x, mask, fill):
    if mask.shape[0] == x.shape[0]:
        return jnp.where(mask, x, fill)
    nq, t = mask.shape
    return jnp.where(mask[None], x.reshape(x.shape[0] // nq, nq, t), fill).reshape(x.shape)


def _flash_update(s, mask, pv, m_sc, l_sc, acc_sc, idx=None):
    at = (lambda r: r.at[idx]) if idx is not None else (lambda r: r)
    if mask is not None:
        s = _masked(s, mask, NEG)
    m_prev = at(m_sc)[...]
    m_new = jnp.maximum(m_prev, jnp.max(s, axis=-1, keepdims=True))
    alpha = jnp.exp(m_prev - m_new)
    p = jnp.exp(s - m_new)
    if mask is not None:
        p = _masked(p, mask, 0.0)
    at(l_sc)[...] = alpha * at(l_sc)[...] + jnp.sum(p, axis=-1, keepdims=True)
    at(acc_sc)[...] = alpha * at(acc_sc)[...] + pv(p)
    at(m_sc)[...] = m_new


def _flash_out(l_sc, acc_sc, idx=None):
    at = (lambda r: r.at[idx]) if idx is not None else (lambda r: r)
    return at(acc_sc)[...] / jnp.maximum(at(l_sc)[...], 1e-30)


def _row_tok(nrows, nq):
    return lax.broadcasted_iota(jnp.int32, (nrows, 1), 0) % nq


def _cat(vals, axis):
    return vals[0] if len(vals) == 1 else jnp.concatenate(vals, axis=axis)


def _ld(ref):
    return ref[(0,) * (len(ref.shape) - 2)]


def _lane():
    return lax.broadcasted_iota(jnp.int32, (1, LANE), 1)


def _rows_from_heads(q, place):
    lane = _lane()
    out = []
    for h in range(N_HEADS):
        x = q[:, (h // 2) * LANE:(h // 2 + 1) * LANE]
        if h % 2 != place[h]:
            x = pltpu.roll(x, HEAD_DIM, 1)
        out.append(jnp.where((lane < HEAD_DIM) if place[h] == 0 else (lane >= HEAD_DIM), x, 0.0))
    return jnp.concatenate(out, axis=0)


def _heads_from_rows(o, src, nq):
    lane = _lane()
    chunks = []
    for c in range(N_HEADS // 2):
        lo = o[(2 * c) * nq:(2 * c + 1) * nq]
        hi = o[(2 * c + 1) * nq:(2 * c + 2) * nq]
        if src[2 * c] == 1:
            lo = pltpu.roll(lo, HEAD_DIM, 1)
        if src[2 * c + 1] == 0:
            hi = pltpu.roll(hi, HEAD_DIM, 1)
        chunks.append(jnp.where(lane < HEAD_DIM, lo, hi))
    return jnp.concatenate(chunks, axis=1)


MOBA_PLACE = tuple(h // (N_HEADS // 2) for h in range(N_HEADS))
NSA_PLACE = (0,) * N_HEADS
NSA_SRC = (1,) * N_HEADS


class _Geom:
    def __init__(self, sample, batch, seq, row0, nq, tok_per_ref, n_ref, n_chunk, past_len):
        self.sample = sample
        self.batch = batch
        self.seq = seq
        self.row0 = row0
        self.nq = nq
        self.n_qb = seq // nq
        self.tok_per_ref = tok_per_ref
        self.n_ref = n_ref
        self.tc = tok_per_ref * n_ref
        self.n_chunk = n_chunk
        self.past_len = past_len

    def q_base(self, qb):
        return self.past_len if self.sample else qb * self.nq

    def q_block(self, i, qb):
        return (self.row0 + i * self.seq) // self.nq + qb

    def out_block(self, i, qb):
        return i * self.n_qb + qb

    def chunk_live(self, qb, c):
        return c * self.tc <= qb * self.nq + self.nq - 1

    def chunk_block(self, i, qb, c):
        last = jnp.minimum((qb * self.nq + self.nq - 1) // self.tc, self.n_chunk - 1)
        return (self.row0 + i * self.seq) // self.tc + jnp.minimum(c, last)

    def new_block(self, i):
        return (self.row0 + i * self.seq) // LANE

    def new_off(self, i):
        return (self.row0 + i * self.seq) % LANE

    def new_kpos(self, i, live):
        rel = _lane() - self.new_off(i)
        return jnp.where((rel >= 0) & (rel < self.seq) & live, self.past_len + rel, FAR)


def _page_spec(geom, li, rows, row_blk, j, n_grid):
    n = geom.n_ref
    if n_grid == 3:
        return pl.BlockSpec((1, 1, rows, LANE), lambda i, q, c, pt: (li, pt[i, c * n + j], row_blk, 0))
    return pl.BlockSpec((1, 1, rows, LANE), lambda i, c, pt: (li, pt[i, c * n + j], row_blk, 0))


def _mask_tail(s, kpos, qpos, start):
    if start == 0:
        return jnp.where(kpos <= qpos, s, NEG)
    tail = jnp.where(kpos[:, start:] <= qpos, s[:, start:], NEG)
    return jnp.concatenate([s[:, :start], tail], axis=1)


def _mla_kernel(geom, pt_ref, qa_ref, qpe_ref, *refs):
    n = geom.n_ref
    ckv_refs, kpe_refs = list(refs[:n]), list(refs[n:2 * n])
    refs = refs[2 * n:]
    if geom.sample:
        ckv_refs.append(refs[0])
        kpe_refs.append(refs[1])
        cosn_ref, sinn_ref = refs[2:4]
        refs = refs[4:]
    cos_ref, sin_ref, wukt_ref, gpe_ref, o_ref, m_sc, l_sc, acc_sc = refs
    nq, tc = geom.nq, geom.tc
    nrows = MLA_HEADS * nq
    i, qb, c = pl.program_id(0), pl.program_id(1), pl.program_id(2)
    qpos = geom.q_base(qb) + _row_tok(nrows, nq)
    lane = _lane()

    def process():
        qa = jnp.concatenate([qa_ref[:, h * LANE:(h + 1) * LANE] for h in range(MLA_HEADS)], axis=0)
        qpe = []
        for h in range(MLA_HEADS):
            x = qpe_ref[:, (h // 4) * LANE:(h // 4 + 1) * LANE]
            if h % 4:
                x = pltpu.roll(x, LANE - (h % 4) * MLA_ROPE, 1)
            qpe.append(jnp.where(lane < MLA_ROPE, x, 0.0))
        qpe = jnp.concatenate(qpe, axis=0)
        ckv = _cat([_ld(r) for r in ckv_refs], 0)
        kpet = _cat([_ld(r) for r in kpe_refs], 1)
        cos, sin = cos_ref[...], sin_ref[...]
        kpos = c * tc + lax.broadcasted_iota(jnp.int32, (1, tc), 1)
        if geom.sample:
            cos = jnp.concatenate([cos, cosn_ref[0]], axis=1)
            sin = jnp.concatenate([sin, sinn_ref[0]], axis=1)
            kpos = jnp.concatenate([kpos, geom.new_kpos(i, c == geom.n_chunk - 1)], axis=1)
        t = ckv.shape[0]
        ckv_b = ckv.astype(BF16)
        knt = lax.dot_general(wukt_ref[...], ckv_b, (((1,), (1,)), ((), ())),
                              preferred_element_type=F32)
        ss = jnp.sum((knt * knt).reshape(MLA_HEADS, MLA_NOPE, t), axis=1)
        pe2 = jnp.sum(kpet * kpet, axis=0, keepdims=True)
        rt = lax.rsqrt((ss + pe2) * (1.0 / MLA_QK) + EPS)
        kg = kpet * gpe_ref[...]
        x1, x2 = kg[:MLA_ROPE // 2], kg[MLA_ROPE // 2:]
        kr = jnp.concatenate([x1 * cos - x2 * sin, x2 * cos + x1 * sin,
                              jnp.zeros((LANE - MLA_ROPE, t), F32)], axis=0)
        s = _dot_nt(qa, ckv_b) + _dot(qpe, kr)
        s = jnp.concatenate([s[h * nq:(h + 1) * nq] * rt[h:h + 1] for h in range(MLA_HEADS)], axis=0)
        s = _mask_tail(s, kpos, qpos, tc if geom.sample else 0)
        _flash_update(s, None, lambda p: _dot(p, ckv_b), m_sc, l_sc, acc_sc)

    @pl.when(c == 0)
    def _():
        _flash_init(m_sc, l_sc, acc_sc)

    if geom.sample:
        process()
    else:
        pl.when(geom.chunk_live(qb, c))(process)

    @pl.when(c == geom.n_chunk - 1)
    def _():
        lat = _flash_out(l_sc, acc_sc)
        for h in range(MLA_HEADS):
            o_ref[:, h * LANE:(h + 1) * LANE] = lat[h * nq:(h + 1) * nq]


def mla_attention(geom, li, pt, r, pools, tabs, lw):
    n, tc, tpr, nq = geom.n_ref, geom.tc, geom.tok_per_ref, geom.nq
    cos_t, sin_t, cos_n, sin_n = tabs
    in_specs = [pl.BlockSpec((nq, 8 * LANE), lambda i, q, c, pt: (geom.q_block(i, q), 0)),
                pl.BlockSpec((nq, 2 * LANE), lambda i, q, c, pt: (geom.q_block(i, q), 0))]
    args = [r["qa"], r["qpe"]]
    if geom.sample:
        in_specs += [_page_spec(geom, li, tpr, 0, j, 3) for j in range(n)]
        in_specs += [_page_spec(geom, li, MLA_ROPE, 0, j, 3) for j in range(n)]
        args += [pools[0]] * n + [pools[1]] * n
        noff = LANE // geom.seq
        in_specs += [pl.BlockSpec((LANE, LANE), lambda i, q, c, pt: (geom.new_block(i), 0)),
                     pl.BlockSpec((MLA_ROPE, LANE), lambda i, q, c, pt: (0, geom.new_block(i))),
                     pl.BlockSpec((1, MLA_ROPE // 2, LANE), lambda i, q, c, pt: (geom.new_off(i) // geom.seq, 0, 0)),
                     pl.BlockSpec((1, MLA_ROPE // 2, LANE), lambda i, q, c, pt: (geom.new_off(i) // geom.seq, 0, 0))]
        args += [r["ckv"], r["last_t"], cos_n, sin_n]
        assert cos_n.shape[0] == noff
        tab_spec = pl.BlockSpec((MLA_ROPE // 2, tc), lambda i, q, c, pt: (0, c))
    else:
        in_specs += [pl.BlockSpec((tc, LANE), lambda i, q, c, pt: (geom.chunk_block(i, q, c), 0)),
                     pl.BlockSpec((MLA_ROPE, tc), lambda i, q, c, pt: (0, geom.chunk_block(i, q, c)))]
        args += [r["ckv"], r["last_t"]]
        tab_spec = pl.BlockSpec((MLA_ROPE // 2, tc), lambda i, q, c, pt: (0, geom.chunk_block(0, q, c)))
    in_specs += [tab_spec, tab_spec,
                 pl.BlockSpec(lw["wukt"].shape, lambda i, q, c, pt: (0, 0)),
                 pl.BlockSpec((MLA_ROPE, 1), lambda i, q, c, pt: (0, 0))]
    args += [cos_t, sin_t, lw["wukt"], lw["k_gpe"]]
    nrows = MLA_HEADS * nq
    return pl.pallas_call(
        functools.partial(_mla_kernel, geom),
        grid_spec=pltpu.PrefetchScalarGridSpec(
            num_scalar_prefetch=1, grid=(geom.batch, geom.n_qb, geom.n_chunk), in_specs=in_specs,
            out_specs=pl.BlockSpec((nq, 8 * LANE), lambda i, q, c, pt: (geom.out_block(i, q), 0)),
            scratch_shapes=[pltpu.VMEM((nrows, 1), F32), pltpu.VMEM((nrows, 1), F32),
                            pltpu.VMEM((nrows, LANE), F32)]),
        out_shape=jax.ShapeDtypeStruct((geom.batch * geom.seq, 8 * LANE), F32),
        compiler_params=_params(("arbitrary", "arbitrary", "arbitrary")),
        name="mla_sample" if geom.sample else "mla_prompt",
    )(pt, *args)


def _diff_kernel(geom, lam_init, pt_ref, q_ref, *refs):
    n = geom.n_ref
    n_kv = n + 1 if geom.sample else n
    kv_refs = refs[:n_kv]
    lam_ref, o_ref, m_sc, l_sc, acc_sc = refs[n_kv:]
    nq, tc, tpr = geom.nq, geom.tc, geom.tok_per_ref
    nrows = 4 * nq
    i, qb, c = pl.program_id(0), pl.program_id(1), pl.program_id(2)
    qpos = geom.q_base(qb) + _row_tok(nrows, nq)
    lane = _lane()

    def rows_of(ref, k):
        rows = pl.ds(k, ref.shape[-2] // 4, stride=4)
        return ref[rows, :] if len(ref.shape) == 2 else ref[0, 0, rows, :]

    def process():
        kpos = c * tc + lax.broadcasted_iota(jnp.int32, (1, tc), 1)
        if geom.sample:
            kpos = jnp.concatenate([kpos, geom.new_kpos(i, c == geom.n_chunk - 1)], axis=1)
        for g in range(2):
            qg = []
            for comp in range(2):
                for rr in range(2):
                    x = q_ref[:, (2 * g + rr) * LANE:(2 * g + rr + 1) * LANE]
                    qg.append(jnp.where((lane < HEAD_DIM) if comp == 0 else (lane >= HEAD_DIM), x, 0.0))
            qg = jnp.concatenate(qg, axis=0)
            k_b = _cat([rows_of(r, g) for r in kv_refs], 0).astype(BF16)
            v_b = _cat([rows_of(r, 2 + g) for r in kv_refs], 0).astype(BF16)
            s = _mask_tail(_dot_nt(qg, k_b), kpos, qpos, tc if geom.sample else 0)
            _flash_update(s, None, lambda p: _dot(p, v_b), m_sc, l_sc, acc_sc, idx=g)

    @pl.when(c == 0)
    def _():
        _flash_init(m_sc, l_sc, acc_sc)

    if geom.sample:
        process()
    else:
        pl.when(geom.chunk_live(qb, c))(process)

    @pl.when(c == geom.n_chunk - 1)
    def _():
        lv = lam_ref[...]
        lam = (jnp.exp(jnp.sum(lv[0:1] * lv[1:2], axis=-1, keepdims=True))
               - jnp.exp(jnp.sum(lv[2:3] * lv[3:4], axis=-1, keepdims=True)) + lam_init)
        for g in range(2):
            o = _flash_out(l_sc, acc_sc, idx=g)
            o = o[:2 * nq] - lam * o[2 * nq:]
            for rr in range(2):
                o_ref[:, (2 * g + rr) * LANE:(2 * g + rr + 1) * LANE] = o[rr * nq:(rr + 1) * nq]


def diff_attention(geom, li, pt, r, diff_rows, pool, lam_vecs):
    n, tc, tpr, nq = geom.n_ref, geom.tc, geom.tok_per_ref, geom.nq
    in_specs = [pl.BlockSpec((nq, MIX_W), lambda i, q, c, pt: (geom.q_block(i, q), 0))]
    args = [r["q_diff"]]
    if geom.sample:
        in_specs += [_page_spec(geom, li, 4 * tpr, 0, j, 3) for j in range(n)]
        args += [pool] * n
        in_specs.append(pl.BlockSpec((4 * LANE, LANE), lambda i, q, c, pt: (geom.new_block(i), 0)))
    else:
        in_specs.append(pl.BlockSpec((4 * tc, LANE), lambda i, q, c, pt: (geom.chunk_block(i, q, c), 0)))
    args.append(diff_rows)
    in_specs.append(pl.BlockSpec((4, HEAD_DIM), lambda i, q, c, pt: (0, 0)))
    args.append(lam_vecs)
    return pl.pallas_call(
        functools.partial(_diff_kernel, geom, _lambda_init(li)),
        grid_spec=pltpu.PrefetchScalarGridSpec(
            num_scalar_prefetch=1, grid=(geom.batch, geom.n_qb, geom.n_chunk), in_specs=in_specs,
            out_specs=pl.BlockSpec((nq, MIX_W), lambda i, q, c, pt: (geom.out_block(i, q), 0)),
            scratch_shapes=[pltpu.VMEM((2, 4 * nq, 1), F32), pltpu.VMEM((2, 4 * nq, 1), F32),
                            pltpu.VMEM((2, 4 * nq, LANE), F32)]),
        out_shape=jax.ShapeDtypeStruct((geom.batch * geom.seq, MIX_W), F32),
        compiler_params=_params(("arbitrary", "arbitrary", "arbitrary")),
        name="diff_sample" if geom.sample else "diff_prompt",
    )(pt, *args)


def _moba_kernel(geom, nb_cand, pt_ref, q_ref, *refs):
    n = geom.n_ref
    kv_refs = refs[:n]
    refs = refs[n:]
    if geom.sample:
        new_ref = refs[0]
        refs = refs[1:]
    o_ref, m_sc, l_sc, km_sc, o_sc = refs
    nq, tc = geom.nq, geom.tc
    nrows = N_HEADS * nq
    nbc = tc // MOBA_BLOCK
    i, qb, c = pl.program_id(0), pl.program_id(1), pl.program_id(2)
    q_base = geom.q_base(qb)
    qpos = q_base + _row_tok(nrows, nq)
    cur = q_base // MOBA_BLOCK
    lane = _lane()
    q = _rows_from_heads(q_ref[...], MOBA_PLACE)

    def block(kvt, blk, kpos):
        kt, vt = kvt[:LANE], kvt[LANE:]
        s = _dot(q, kt)
        if kpos is not None:
            s = jnp.where(kpos <= qpos, s, NEG)
        mb = jnp.max(s, axis=-1, keepdims=True)
        p = jnp.exp(s - mb)
        here = lane == blk
        m_sc[...] = jnp.where(here, mb, m_sc[...])
        l_sc[...] = jnp.where(here, jnp.sum(p, axis=-1, keepdims=True), l_sc[...])
        km_sc[...] = jnp.where(here, jnp.sum(kt, axis=-1, keepdims=True) * (1.0 / MOBA_BLOCK), km_sc[...])
        o_sc[blk] = _dot_nt(p, vt)

    @pl.when((i == 0) & (qb == 0) & (c == 0))
    def _():
        m_sc[...] = jnp.full(m_sc.shape, NEG, F32)
        l_sc[...] = jnp.zeros(l_sc.shape, F32)
        km_sc[...] = jnp.zeros(km_sc.shape, F32)
        if not geom.sample:
            o_sc[...] = jnp.zeros(o_sc.shape, F32)

    if geom.sample:
        block(new_ref[...], nb_cand, geom.new_kpos(i, True))
        kvt = _cat([_ld(r) for r in kv_refs], 1)
        for j in range(nbc):
            block(kvt[:, j * MOBA_BLOCK:(j + 1) * MOBA_BLOCK], c * nbc + j, None)
    else:
        for j in range(nbc):
            blk = c * nbc + j

            @pl.when(blk <= cur)
            def _():
                kvt = kv_refs[0][:, j * MOBA_BLOCK:(j + 1) * MOBA_BLOCK]
                kpos = blk * MOBA_BLOCK + lax.broadcasted_iota(jnp.int32, (1, MOBA_BLOCK), 1)
                block(kvt, blk, kpos)

    @pl.when(c == geom.n_chunk - 1)
    def _():
        gate = _dot_xx(q, km_sc[...])
        past = lane < cur
        rank = jnp.zeros(gate.shape, jnp.int32)
        for mth in range(nb_cand):
            gm = gate[:, mth:mth + 1]
            ahead = (gm > gate) | ((gm == gate) & (mth < lane))
            if not geom.sample:
                ahead = ahead & (mth < cur)
            rank = rank + ahead.astype(jnp.int32)
        sel = (past & (rank < MOBA_TOPK) & (jnp.abs(gate) < jnp.inf)) | (lane == cur)
        mm_ = jnp.where(sel, m_sc[...], NEG)
        mtop = jnp.max(mm_, axis=-1, keepdims=True)
        w = jnp.where(sel, jnp.exp(mm_ - mtop), 0.0)
        den = jnp.sum(w * l_sc[...], axis=-1, keepdims=True)
        acc = jnp.zeros((nrows, LANE), F32)
        for blk in range(nb_cand + 1 if geom.sample else nb_cand):
            acc = acc + w[:, blk:blk + 1] * o_sc[blk]
        o_ref[...] = _heads_from_rows(acc / jnp.maximum(den, 1e-30), MOBA_PLACE, nq)


def moba_attention(geom, li, pt, r, pool):
    n, tc, tpr, nq = geom.n_ref, geom.tc, geom.tok_per_ref, geom.nq
    nrows = N_HEADS * nq
    nb_cand = geom.n_chunk * (tc // MOBA_BLOCK)
    assert nb_cand + 1 <= LANE
    in_specs = [pl.BlockSpec((nq, MIX_W), lambda i, q, c, pt: (geom.q_block(i, q), 0))]
    args = [r["q_moba"]]
    if geom.sample:
        in_specs += [_page_spec(geom, li, 2 * LANE, 0, j, 3) for j in range(n)]
        args += [pool] * n
        in_specs.append(pl.BlockSpec((2 * LANE, LANE), lambda i, q, c, pt: (0, geom.new_block(i))))
    else:
        in_specs.append(pl.BlockSpec((2 * LANE, tc), lambda i, q, c, pt: (0, geom.chunk_block(i, q, c))))
    args.append(r["moba_kv_t"])
    return pl.pallas_call(
        functools.partial(_moba_kernel, geom, nb_cand),
        grid_spec=pltpu.PrefetchScalarGridSpec(
            num_scalar_prefetch=1, grid=(geom.batch, geom.n_qb, geom.n_chunk), in_specs=in_specs,
            out_specs=pl.BlockSpec((nq, MIX_W), lambda i, q, c, pt: (geom.out_block(i, q), 0)),
            scratch_shapes=[pltpu.VMEM((nrows, LANE), F32), pltpu.VMEM((nrows, LANE), F32),
                            pltpu.VMEM((LANE, LANE), F32), pltpu.VMEM((nb_cand + 1, nrows, LANE), F32)]),
        out_shape=jax.ShapeDtypeStruct((geom.batch * geom.seq, MIX_W), F32),
        compiler_params=_params(("arbitrary", "arbitrary", "arbitrary")),
        name="moba_sample" if geom.sample else "moba_prompt",
    )(pt, *args)


def _gelu_tanh(x):
    return 0.5 * x * (1.0 + jnp.tanh(math.sqrt(2.0 / math.pi) * (x + 0.044715 * x * x * x)))


def _compress_kernel(geom, nseg, pt_ref, *refs):
    n = geom.n_ref
    kv_refs = refs[:n]
    (ptop_ref, pbot_ref, wtop_ref, wbot_ref, b1_ref, w2_ref, kg_ref, c_ref, sa_ref, sb_ref,
     o_ref, x_sc, a_sc, b_sc) = refs[n:]
    tc, tpr = geom.tc, geom.tok_per_ref
    segc = tc // NSA_CMP_STRIDE
    c = pl.program_id(1)
    for j, r in enumerate(kv_refs):
        x_sc[j * tpr:(j + 1) * tpr, :] = _ld(r).T if geom.sample else _ld(r)
    u = jnp.concatenate([x_sc[pl.ds(r, segc, stride=NSA_CMP_STRIDE), :] for r in range(NSA_CMP_STRIDE)], axis=1)
    row0 = pl.multiple_of(c * segc, 8)
    a_sc[pl.ds(row0, segc), :] = _dot(u + ptop_ref[...], wtop_ref[...])
    b_sc[pl.ds(row0, segc), :] = _dot(u + pbot_ref[...], wbot_ref[...])

    @pl.when(c == geom.n_chunk - 1)
    def _():
        hid = a_sc[...] + pltpu.roll(b_sc[...], nseg - 1, 0) + b1_ref[...]
        kv = _dot(_gelu_tanh(hid), w2_ref[...])
        lane = _lane()
        ri = lax.broadcasted_iota(jnp.int32, (LANE, LANE), 0) // HEAD_DIM
        ci = lax.broadcasted_iota(jnp.int32, (LANE, LANE), 1) // HEAD_DIM
        ss = _dot_x(kv * kv, (ri == ci).astype(BF16))
        y = kv * lax.rsqrt(ss * (1.0 / HEAD_DIM) + EPS) * kg_ref[...]
        y = _rope_apply(y, c_ref[...], sa_ref[...], sb_ref[...], ROT_DIM // 2)
        o_ref[0] = jnp.where(lane < HEAD_DIM, y, kv)


def nsa_compress(geom, li, pt, r, pool, nseg, lw, tabs):
    n, tc, tpr = geom.n_ref, geom.tc, geom.tok_per_ref
    if geom.sample:
        in_specs = [_page_spec(geom, li, LANE, 0, j, 2) for j in range(n)]
        args = [pool] * n
    else:
        in_specs = [pl.BlockSpec((tc, LANE), lambda i, c, pt: ((geom.row0 + i * geom.seq) // tc + c, 0))]
        args = [r["nsa_kv"]]
    consts = (lw["cmp_ptop"], lw["cmp_pbot"], lw["cmp_wtop"], lw["cmp_wbot"], lw["cmp_b1"], lw["cmp_w2"],
              lw["cmp_kg"]) + tuple(tabs)
    in_specs += [pl.BlockSpec(a.shape, lambda i, c, pt: (0, 0)) for a in consts]
    return pl.pallas_call(
        functools.partial(_compress_kernel, geom, nseg),
        grid_spec=pltpu.PrefetchScalarGridSpec(
            num_scalar_prefetch=1, grid=(geom.batch, geom.n_chunk), in_specs=in_specs,
            out_specs=pl.BlockSpec((1, nseg, LANE), lambda i, c, pt: (i, 0, 0)),
            scratch_shapes=[pltpu.VMEM((tc, LANE), F32), pltpu.VMEM((nseg, 2 * LANE), F32),
                            pltpu.VMEM((nseg, 2 * LANE), F32)]),
        out_shape=jax.ShapeDtypeStruct((geom.batch, nseg, LANE), F32),
        compiler_params=_params(("arbitrary", "arbitrary")),
        name="nsa_compress_sample" if geom.sample else "nsa_compress_prompt",
    )(pt, *args, *consts)


def _win_kernel(geom, pt_ref, q_ref, *refs):
    o_ref = refs[-1]
    kv_refs = refs[:-1]
    nq = geom.nq
    nrows = N_HEADS * nq
    i, qb = pl.program_id(0), pl.program_id(1)
    q_base = geom.q_base(qb)
    qpos = q_base + _row_tok(nrows, nq)
    kvt = jnp.concatenate([_ld(r) for r in kv_refs], axis=1)
    if geom.sample:
        kpos = jnp.concatenate([geom.past_len - NSA_WINDOW + lax.broadcasted_iota(jnp.int32, (1, NSA_WINDOW), 1),
                                geom.new_kpos(i, True)], axis=1)
    else:
        kpos = q_base - NSA_WINDOW + lax.broadcasted_iota(jnp.int32, (1, kvt.shape[1]), 1)
    d = qpos - kpos
    mask = (d >= 0) & (d < NSA_WINDOW) & (kpos >= 0)
    s = jnp.where(mask, _dot(_rows_from_heads(q_ref[...], NSA_PLACE), kvt), NEG)
    p = jnp.where(mask, jnp.exp(s - jnp.max(s, axis=-1, keepdims=True)), 0.0)
    o = _dot_nt(p, kvt)
    o_ref[0, 0] = o / jnp.maximum(jnp.sum(p, axis=-1, keepdims=True), 1e-30)


def nsa_window(geom, li, pt, r, win_state):
    nq = geom.nq
    nrows = N_HEADS * nq
    in_specs = [pl.BlockSpec((nq, MIX_W), lambda i, q, pt: (geom.q_block(i, q), 0))]
    if geom.sample:
        in_specs += [pl.BlockSpec((1, 1, LANE, NSA_WINDOW), lambda i, q, pt: (li, i, 0, 0)),
                     pl.BlockSpec((LANE, LANE), lambda i, q, pt: (0, geom.new_block(i)))]
        args = [win_state, r["nsa_win_t"]]
    else:
        nback = NSA_WINDOW // nq
        in_specs += [pl.BlockSpec((LANE, nq),
                                  functools.partial(lambda i, q, pt, j: (0, geom.q_block(i, 0) + jnp.maximum(q - nback + j, 0)),
                                                    j=j))
                     for j in range(nback + 1)]
        args = [r["nsa_win_t"]] * (nback + 1)
    return pl.pallas_call(
        functools.partial(_win_kernel, geom),
        grid_spec=pltpu.PrefetchScalarGridSpec(
            num_scalar_prefetch=1, grid=(geom.batch, geom.n_qb), in_specs=in_specs,
            out_specs=pl.BlockSpec((1, 1, nrows, LANE), lambda i, q, pt: (i, q, 0, 0))),
        out_shape=jax.ShapeDtypeStruct((geom.batch, geom.n_qb, nrows, LANE), F32),
        compiler_params=_params(("arbitrary", "arbitrary")),
        name="nsa_window_sample" if geom.sample else "nsa_window_prompt",
    )(pt, r["q_nsa"], *args)


def _nsa_kernel(geom, nc, nsb, pt_ref, q_ref, kvc_ref, gate_ref, owin_ref, *refs):
    n = geom.n_ref
    n_kv = n + 1 if geom.sample else n
    kv_refs = refs[:n_kv]
    o_ref, m_sc, l_sc, acc_sc, ocmp_sc, sel_sc = refs[n_kv:]
    nq, tc = geom.nq, geom.tc
    nrows = N_HEADS * nq
    nsbp = sel_sc.shape[1]
    i, qb, c = pl.program_id(0), pl.program_id(1), pl.program_id(2)
    q_base = geom.q_base(qb)
    qtok = q_base + lax.broadcasted_iota(jnp.int32, (nq, 1), 0)
    q = _rows_from_heads(q_ref[...], NSA_PLACE)

    def select_and_compress():
        kvc = kvc_ref[0]
        ncp = kvc.shape[0]
        ci = lax.broadcasted_iota(jnp.int32, (1, ncp), 1)
        cend = ci * NSA_CMP_STRIDE + (NSA_CMP_LEN - 1)
        mask = ((cend <= qtok) & (ci < nc))[None]
        s = _dot_nt(q, kvc).reshape(N_HEADS, nq, ncp)
        s = jnp.where(mask, s, NEG)
        e = jnp.where(mask, jnp.exp(s - jnp.max(s, axis=-1, keepdims=True)), 0.0)
        p = e / jnp.maximum(jnp.sum(e, axis=-1, keepdims=True), 1e-30)
        ocmp_sc[...] = _dot(p.reshape(nrows, ncp), kvc)
        cs = lax.broadcasted_iota(jnp.int32, (ncp, 1), 0) * NSA_CMP_STRIDE
        bs = lax.broadcasted_iota(jnp.int32, (1, nsbp), 1) * NSA_SEL_BLOCK
        covers = ((cs < bs + NSA_SEL_BLOCK) & (cs + NSA_CMP_LEN > bs)).astype(BF16)
        imp = _dot_x(jnp.sum(p, axis=0), covers)
        blk = lax.broadcasted_iota(jnp.int32, (1, nsbp), 1)
        curb = qtok // NSA_SEL_BLOCK
        forced = (blk == curb) | (blk == 0)
        score = jnp.where(forced, SEL_FORCE, jnp.where(blk <= curb, imp, -jnp.inf))
        rank = jnp.zeros(score.shape, jnp.int32)
        for mth in range(nsb):
            sm = score[:, mth:mth + 1]
            rank = rank + ((sm > score) | ((sm == score) & (mth < blk))).astype(jnp.int32)
        sel_sc[...] = ((rank < NSA_TOPN) & (score > -jnp.inf)).astype(F32)

    def process():
        kvt = _cat([_ld(r) for r in kv_refs], 1)
        kpos = c * tc + lax.broadcasted_iota(jnp.int32, (1, tc), 1)
        if geom.sample:
            kpos = jnp.concatenate([kpos, geom.new_kpos(i, c == geom.n_chunk - 1)], axis=1)
        expand = (lax.broadcasted_iota(jnp.int32, (nsbp, 1), 0) == kpos // NSA_SEL_BLOCK).astype(BF16)
        picked = jnp.dot(sel_sc[...].astype(BF16), expand, preferred_element_type=F32) > 0.5
        mask = picked & (kpos <= qtok)
        _flash_update(_dot(q, kvt), mask, lambda p: _dot_nt(p, kvt), m_sc, l_sc, acc_sc)

    @pl.when(c == 0)
    def _():
        _flash_init(m_sc, l_sc, acc_sc)
        select_and_compress()

    if geom.sample:
        process()
    else:
        pl.when(geom.chunk_live(qb, c))(process)

    @pl.when(c == geom.n_chunk - 1)
    def _():
        g = gate_ref[...]
        osel = _flash_out(l_sc, acc_sc)
        outs = []
        for h in range(N_HEADS):
            sl = slice(h * nq, (h + 1) * nq)
            g0 = GATE_LANE0 + 3 * h
            outs.append(g[:, g0:g0 + 1] * ocmp_sc[sl] + g[:, g0 + 1:g0 + 2] * osel[sl]
                        + g[:, g0 + 2:g0 + 3] * owin_ref[0, 0, sl])
        o_ref[...] = _heads_from_rows(jnp.concatenate(outs, axis=0), NSA_SRC, nq)


def nsa_attention(geom, li, pt, r, kvc, o_win, pool, nc, nsb):
    n, tc, tpr, nq = geom.n_ref, geom.tc, geom.tok_per_ref, geom.nq
    nrows = N_HEADS * nq
    nsbp = -(-nsb // LANE) * LANE
    in_specs = [pl.BlockSpec((nq, MIX_W), lambda i, q, c, pt: (geom.q_block(i, q), 0)),
                pl.BlockSpec((1,) + kvc.shape[1:], lambda i, q, c, pt: (i, 0, 0)),
                pl.BlockSpec((nq, LANE), lambda i, q, c, pt: (geom.q_block(i, q), 0)),
                pl.BlockSpec((1, 1, nrows, LANE), lambda i, q, c, pt: (i, q, 0, 0))]
    args = [r["q_nsa"], kvc, r["last"], o_win]
    if geom.sample:
        in_specs += [_page_spec(geom, li, LANE, 1, j, 3) for j in range(n)]
        args += [pool] * n
        in_specs.append(pl.BlockSpec((LANE, LANE), lambda i, q, c, pt: (1, geom.new_block(i))))
    else:
        in_specs.append(pl.BlockSpec((LANE, tc), lambda i, q, c, pt: (1, geom.chunk_block(i, q, c))))
    args.append(r["nsa_kv_t"])
    return pl.pallas_call(
        functools.partial(_nsa_kernel, geom, nc, nsb),
        grid_spec=pltpu.PrefetchScalarGridSpec(
            num_scalar_prefetch=1, grid=(geom.batch, geom.n_qb, geom.n_chunk), in_specs=in_specs,
            out_specs=pl.BlockSpec((nq, MIX_W), lambda i, q, c, pt: (geom.out_block(i, q), 0)),
            scratch_shapes=[pltpu.VMEM((nrows, 1), F32), pltpu.VMEM((nrows, 1), F32),
                            pltpu.VMEM((nrows, LANE), F32), pltpu.VMEM((nrows, LANE), F32),
                            pltpu.VMEM((nq, nsbp), F32)]),
        out_shape=jax.ShapeDtypeStruct((geom.batch * geom.seq, MIX_W), F32),
        compiler_params=_params(("arbitrary", "arbitrary", "arbitrary")),
        name="nsa_sample" if geom.sample else "nsa_prompt",
    )(pt, *args)


def _rope_tables(pos, half, theta, period, n_rep):
    inv = theta ** (-jnp.arange(half, dtype=F32) / half)
    ang = pos.astype(F32)[:, None] * inv[None, :]
    cos, sin = jnp.cos(ang), jnp.sin(ang)
    t = pos.shape[0]
    ones = jnp.ones((t, period - 2 * half), F32)
    zeros = jnp.zeros((t, period - 2 * half), F32)
    zh = jnp.zeros((t, half), F32)
    c = jnp.concatenate([cos, cos, ones], axis=1)
    sa = jnp.concatenate([-sin, zh, zeros], axis=1)
    sb = jnp.concatenate([zh, sin, zeros], axis=1)
    return tuple(jnp.tile(a, (1, n_rep)) for a in (c, sa, sb))


def _layer_weights(li, p):
    w_in = p["w_in"][li]
    d = w_in.shape[0]
    wa = jnp.concatenate([w_in[:, 0:512], w_in[:, 544:3232], w_in[:, 512:544], w_in[:, 3232:3256],
                          jnp.zeros((d, LANE - 56), F32)], axis=1).astype(BF16)
    lw = {"wa": wa, "wg": w_in[:, 3256:].astype(BF16)}
    lw["cq_g"] = p["mla_cq_g"][li].reshape(1, -1)
    lw["ckv_g"] = p["mla_ckv_g"][li].reshape(1, -1)
    wuq = p["mla_w_uq"][li]
    lw["wuq"] = jnp.concatenate([wuq[:, :, :MLA_ROPE].reshape(384, -1),
                                 wuq[:, :, MLA_ROPE:].reshape(384, -1)], axis=1).astype(BF16)
    head = jnp.concatenate([jnp.arange(256) // MLA_ROPE, jnp.arange(512) // MLA_NOPE])
    lw["b768"] = (head[:, None] == head[None, :]).astype(BF16)
    qg = p["mla_qn_g"][li]
    lw["q_g768"] = jnp.concatenate([jnp.tile(qg[:MLA_ROPE], 8), jnp.tile(qg[MLA_ROPE:], 8)]).reshape(1, -1)
    kg = p["mla_kn_g"][li]
    wuk = p["mla_w_uk"][li]
    hh = jnp.arange(MLA_HEADS)
    wqa = jnp.zeros((8, MLA_NOPE, 8, LANE), F32)
    wqa = wqa.at[hh, :, hh, :].set(jnp.transpose(wuk, (1, 2, 0)) * kg[MLA_ROPE:][None, :, None])
    lw["wqa"] = wqa.reshape(8 * MLA_NOPE, 8 * LANE).astype(BF16)
    lw["wukt"] = jnp.transpose(wuk, (1, 2, 0)).reshape(8 * MLA_NOPE, LANE).astype(BF16)
    lw["k_gpe"] = kg[:MLA_ROPE].reshape(MLA_ROPE, 1)
    wuv = jnp.zeros((8, LANE, 8, HEAD_DIM), F32)
    wuv = wuv.at[hh, :, hh, :].set(jnp.transpose(p["mla_w_uv"][li], (1, 0, 2)))
    lw["wuv"] = wuv.reshape(8 * LANE, MIX_W).astype(BF16)
    gains = {"q_moba": p["moba_qn_g"][li], "moba_kv": p["moba_kn_g"][li], "q_diff": p["diff_qn_g"][li],
             "diff_kv": p["diff_kn_g"][li], "q_nsa": p["nsa_qn_g"][li]}
    one = jnp.ones((HEAD_DIM,), F32)
    sg = []
    for name, width, en in SLAB_GROUPS:
        for k, e in enumerate(en):
            if name == "nsa_kv":
                sg.append(p["nsa_kn_g"][li, 1] if e else one)
            elif name == "nsa_win":
                sg.append(p["nsa_kn_g"][li, 2] if e else one)
            else:
                sg.append(gains[name] if e else one)
    lw["slab_g"] = jnp.concatenate(sg).reshape(1, -1)
    w1 = p["nsa_cmp_w1"][li].reshape(2, NSA_CMP_LEN, HEAD_DIM, -1)
    hid = w1.shape[-1]
    wkv = jnp.zeros((NSA_CMP_LEN, 2, HEAD_DIM, 2, hid), F32)
    wkv = wkv.at[:, 0, :, 0, :].set(w1[0]).at[:, 1, :, 1, :].set(w1[1])
    wkv = wkv.reshape(NSA_CMP_LEN * 2 * HEAD_DIM, 2 * hid)
    half = NSA_CMP_STRIDE * 2 * HEAD_DIM
    lw["cmp_wtop"] = wkv[:half].astype(BF16)
    lw["cmp_wbot"] = wkv[half:].astype(BF16)
    pos = jnp.transpose(p["nsa_cmp_pos"][li], (1, 0, 2)).reshape(1, NSA_CMP_LEN * 2 * HEAD_DIM)
    lw["cmp_ptop"] = pos[:, :half]
    lw["cmp_pbot"] = pos[:, half:]
    lw["cmp_b1"] = p["nsa_cmp_b1"][li].reshape(1, 2 * hid)
    w2 = p["nsa_cmp_w2"][li]
    w2b = jnp.zeros((2, hid, 2, HEAD_DIM), F32).at[0, :, 0, :].set(w2[0]).at[1, :, 1, :].set(w2[1])
    lw["cmp_w2"] = w2b.reshape(2 * hid, 2 * HEAD_DIM).astype(BF16)
    lw["cmp_kg"] = jnp.concatenate([p["nsa_kn_g"][li, 0], one]).reshape(1, LANE)
    lw["w_branch"] = p["w_branch"][li].astype(BF16)
    lw["w_out"] = p["w_out"][li].astype(BF16)
    lw["w_gu"] = p["ffn_w_gu"][li].astype(BF16)
    lw["w_down"] = p["ffn_w_down"][li].astype(BF16)
    return lw


def _attention_group(geom, geom_whole, li, pt, r, diff_rows, pools, nseg, nsb, tabs, lw, lam_vecs):
    lat = mla_attention(geom, li, pt, r, pools.get("mla"), tabs["mla"], lw)
    o_mla = mm(lat, lw["wuv"])
    o_moba = moba_attention(geom_whole, li, pt, r, pools.get("moba"))
    o_diff = diff_attention(geom, li, pt, r, diff_rows, pools.get("diff"), lam_vecs)
    o_win = nsa_window(geom, li, pt, r, pools.get("win"))
    kvc = nsa_compress(geom_whole, li, pt, r, pools.get("nsa"), nseg, lw, tabs["cmp"])
    o_nsa = nsa_attention(geom, li, pt, r, kvc, o_win, pools.get("nsa"), nseg - 1, nsb)
    return o_mla, o_moba, o_diff, o_nsa


def kernel(x_prompt, x_sample, cache_mla_ckv, cache_mla_kpe, cache_moba_kv, cache_diff_kv, cache_nsa_kv, state_nsa_win, page_table, ln_attn_g, w_in, mla_cq_g, mla_ckv_g, mla_w_uq, mla_qn_g, mla_w_uk, mla_w_uv, mla_kn_g, moba_qn_g, moba_kn_g, diff_qn_g, diff_kn_g, diff_lambda, diff_subln_g, nsa_qn_g, nsa_kn_g, nsa_cmp_pos, nsa_cmp_w1, nsa_cmp_b1, nsa_cmp_w2, w_branch, w_out, ln_ffn_g, ffn_w_gu, ffn_w_down):
    p = dict(w_in=w_in, mla_cq_g=mla_cq_g, mla_ckv_g=mla_ckv_g, mla_w_uq=mla_w_uq, mla_qn_g=mla_qn_g,
             mla_w_uk=mla_w_uk, mla_w_uv=mla_w_uv, mla_kn_g=mla_kn_g, moba_qn_g=moba_qn_g,
             moba_kn_g=moba_kn_g, diff_qn_g=diff_qn_g, diff_kn_g=diff_kn_g, nsa_qn_g=nsa_qn_g,
             nsa_kn_g=nsa_kn_g, nsa_cmp_pos=nsa_cmp_pos, nsa_cmp_w1=nsa_cmp_w1, nsa_cmp_b1=nsa_cmp_b1,
             nsa_cmp_w2=nsa_cmp_w2, w_branch=w_branch, w_out=w_out, ffn_w_gu=ffn_w_gu, ffn_w_down=ffn_w_down)
    bp, tp, d = x_prompt.shape
    bs, ts, _ = x_sample.shape
    depth, n_pool, page = cache_mla_ckv.shape[:3]
    n_pages = page_table.shape[1]
    past = n_pages * page
    mp, ms = bp * tp, bs * ts
    assert page == LANE and past % MOBA_BLOCK == 0 and state_nsa_win.shape[2] == NSA_WINDOW
    assert tp % MOBA_BLOCK == 0 and tp >= NSA_WINDOW and LANE % ts == 0 and ts % 8 == 0
    assert mp % LANE == 0 and ms % LANE == 0
    nq_p = min(Q_BLOCK, tp)
    n_ref = min(PAGES_PER_STEP, n_pages)
    assert n_pages % n_ref == 0 and (n_ref * page) % MOBA_BLOCK == 0
    tc_p = min(512, tp)
    geom_p = _Geom(False, bp, tp, 0, nq_p, tc_p, 1, tp // tc_p, 0)
    geom_pw = _Geom(False, bp, tp, 0, nq_p, tp, 1, 1, 0)
    geom_s = _Geom(True, bs, ts, mp, ts, page, n_ref, n_pages // n_ref, past)
    pt_dummy = jnp.zeros((1, 1), jnp.int32)

    fm = lambda a: jnp.transpose(a, (0, 1, 3, 4, 5, 2)).reshape(a.shape[0], a.shape[1], -1, a.shape[2])
    pools_s = {"moba": fm(cache_moba_kv),
               "nsa": fm(cache_nsa_kv),
               "win": fm(state_nsa_win),
               "mla": (cache_mla_ckv, jnp.transpose(cache_mla_kpe, (0, 1, 3, 2))),
               "diff": cache_diff_kv.reshape(depth, n_pool, 4 * page, LANE)}

    pos_all = jnp.concatenate([jnp.tile(jnp.arange(tp), bp), jnp.tile(past + jnp.arange(ts), bs)])
    row_tabs = (_rope_tables(pos_all, ROT_DIM // 2, ROPE_THETA, HEAD_DIM, 2)
                + _rope_tables(pos_all, MLA_ROPE // 2, MLA_THETA, MLA_ROPE, 4))
    inv_m = (MLA_THETA ** (-jnp.arange(MLA_ROPE // 2, dtype=F32) / (MLA_ROPE // 2)))[:, None]
    ang_p = inv_m * jnp.arange(max(tp, past), dtype=F32)[None, :]
    cos_k, sin_k = jnp.cos(ang_p), jnp.sin(ang_p)
    rel = jnp.arange(LANE)[None, :] - (jnp.arange(LANE // ts) * ts)[:, None]
    ang_n = inv_m[None] * (past + rel).astype(F32)[:, None, :]
    mla_tabs_p = (cos_k[:, :tp], sin_k[:, :tp], None, None)
    mla_tabs_s = (cos_k[:, :past], sin_k[:, :past], jnp.cos(ang_n), jnp.sin(ang_n))

    def cmp_tabs(nseg):
        cend = jnp.arange(nseg) * NSA_CMP_STRIDE + NSA_CMP_LEN - 1
        c, sa, sb = _rope_tables(cend, ROT_DIM // 2, ROPE_THETA, HEAD_DIM, 1)
        one = jnp.ones((nseg, HEAD_DIM), F32)
        zero = jnp.zeros((nseg, HEAD_DIM), F32)
        return (jnp.concatenate([c, one], 1), jnp.concatenate([sa, zero], 1), jnp.concatenate([sb, zero], 1))

    nseg_p, nseg_s = tp // NSA_CMP_STRIDE, past // NSA_CMP_STRIDE
    tabs_p = {"mla": mla_tabs_p, "cmp": cmp_tabs(nseg_p)}
    tabs_s = {"mla": mla_tabs_s, "cmp": cmp_tabs(nseg_s)}

    x = jnp.concatenate([x_prompt.reshape(mp, d), x_sample.reshape(ms, d)], axis=0)
    news = {k: [] for k in ("ckv", "kpe", "moba", "diff", "nsa", "win")}
    for li in range(depth):
        lw = _layer_weights(li, p)
        z = mm_norm(x, ln_attn_g[li], lw["wa"])
        r = token_rows(z, lw, row_tabs)
        for k, src in (("ckv", "ckv"), ("moba", "moba_kv"), ("diff", "diff_kv"), ("nsa", "nsa_kv"),
                       ("win", "nsa_win")):
            news[k].append(r[src])
        news["kpe"].append(r["last"][:, :MLA_ROPE])
        diff_rows = r["diff_kv"].reshape(4 * (mp + ms), LANE)
        op = _attention_group(geom_p, geom_pw, li, pt_dummy, r, diff_rows, {}, nseg_p,
                              -(-tp // NSA_SEL_BLOCK), tabs_p, lw, diff_lambda[li])
        osm = _attention_group(geom_s, geom_s, li, page_table, r, diff_rows, pools_s, nseg_s,
                               -(-(past + ts) // NSA_SEL_BLOCK), tabs_s, lw, diff_lambda[li])
        y = merge_branches(x, ln_attn_g[li], op, osm, diff_subln_g[li], 1.0 - _lambda_init(li), lw["wg"],
                           lw["w_branch"])
        x = mm_res(y, lw["w_out"], x)
        act = mm_norm_swiglu(x, ln_ffn_g[li], lw["w_gu"])
        x = mm_res(act, lw["w_down"], x)

    def stack(k, tail):
        ps = jnp.stack([a[:mp].reshape((bp, tp) + tail) for a in news[k]])
        ss = jnp.stack([a[mp:].reshape((bs, ts) + tail) for a in news[k]])
        return ps, ss

    ckv_p, ckv_s = stack("ckv", (LANE,))
    kpe_p, kpe_s = stack("kpe", (MLA_ROPE,))
    moba_p, moba_s = stack("moba", (2, 2, HEAD_DIM))
    diff_p, diff_s = stack("diff", (2, 2, 2 * HEAD_DIM))
    nsa_p, nsa_s = stack("nsa", (4, 1, HEAD_DIM))
    win_p, win_s = stack("win", (2, 1, HEAD_DIM))
    win_p = win_p[:, :, -NSA_WINDOW:]
    win_s = jnp.concatenate([state_nsa_win, win_s], axis=2)[:, :, -NSA_WINDOW:]
    return (x[:mp].reshape(bp, tp, d), x[mp:].reshape(bs, ts, d),
            ckv_p, ckv_s, kpe_p, kpe_s, moba_p, moba_s, diff_p, diff_s, nsa_p, nsa_s, win_p, win_s)
```

```python
import functools
import math

import jax
import jax.numpy as jnp
from jax import lax
from jax.experimental import pallas as pl
from jax.experimental.pallas import tpu as pltpu

F32 = jnp.float32
BF16 = jnp.bfloat16

EPS = 1e-6
NEG = -1e30
FAR = 1 << 28
LANE = 128
VMEM_LIMIT = 56 * 1024 * 1024

HEAD_DIM = 64
ROT_DIM = 16
ROPE_THETA = 500000.0
MLA_HEADS = 8
MLA_NOPE = 64
MLA_ROPE = 32
MLA_QK = MLA_NOPE + MLA_ROPE
MLA_THETA = 10000.0
MOBA_BLOCK = 256
MOBA_TOPK = 3
NSA_CMP_LEN = 32
NSA_CMP_STRIDE = 16
NSA_SEL_BLOCK = 64
NSA_TOPN = 16
NSA_WINDOW = 512
SEL_FORCE = 1e30
N_HEADS = 8
MIX_W = N_HEADS * HEAD_DIM
Q_BLOCK = 128
PAGES_PER_STEP = 32


def _lambda_init(li):
    return 0.8 - 0.6 * math.exp(-0.3 * li)


def _tile(n, target, mult):
    best = None
    d = mult
    while d <= min(n, target):
        if n % d == 0:
            best = d
        d += mult
    return best if best is not None else n


def _params(sem):
    return pltpu.CompilerParams(dimension_semantics=sem, vmem_limit_bytes=VMEM_LIMIT)


def _dot(a, b):
    return jnp.dot(a.astype(BF16), b.astype(BF16), preferred_element_type=F32)


def _dot_nt(a, b):
    return lax.dot_general(a.astype(BF16), b.astype(BF16), (((1,), (1,)), ((), ())),
                           preferred_element_type=F32)


def _split(a):
    hi = a.astype(BF16)
    lo = (a - hi.astype(F32)).astype(BF16)
    return hi, lo


def _dot_x(a, b_exact):
    hi, lo = _split(a)
    b = b_exact.astype(BF16)
    return (jnp.dot(hi, b, preferred_element_type=F32)
            + jnp.dot(lo, b, preferred_element_type=F32))


def _dot_xx(a, b):
    ah, al = _split(a)
    bh, bl = _split(b)
    return (jnp.dot(ah, bh, preferred_element_type=F32)
            + jnp.dot(ah, bl, preferred_element_type=F32)
            + jnp.dot(al, bh, preferred_element_type=F32))


def _rms(x, g):
    r = lax.rsqrt(jnp.mean(x * x, axis=-1, keepdims=True) + EPS)
    return x * r * g


def _mm_norm_kernel(x_ref, g_ref, w_ref, o_ref, h_sc):
    @pl.when(pl.program_id(1) == 0)
    def _():
        h_sc[...] = _rms(x_ref[...], g_ref[...]).astype(BF16)

    o_ref[...] = jnp.dot(h_sc[...], w_ref[...], preferred_element_type=F32)


def mm_norm(x, g, w):
    m, k = x.shape
    n = w.shape[1]
    tm = _tile(m, 512, 8)
    tn = _tile(n, 2048, LANE)
    return pl.pallas_call(
        _mm_norm_kernel,
        grid=(m // tm, n // tn),
        in_specs=[pl.BlockSpec((tm, k), lambda i, j: (i, 0)),
                  pl.BlockSpec((1, k), lambda i, j: (0, 0)),
                  pl.BlockSpec((k, tn), lambda i, j: (0, j))],
        out_specs=pl.BlockSpec((tm, tn), lambda i, j: (i, j)),
        out_shape=jax.ShapeDtypeStruct((m, n), F32),
        scratch_shapes=[pltpu.VMEM((tm, k), BF16)],
        compiler_params=_params(("parallel", "arbitrary")),
        name="mm_norm",
    )(x, g.reshape(1, k), w)


def _swiglu_kernel(x_ref, g_ref, wg_ref, wu_ref, o_ref, h_sc):
    @pl.when(pl.program_id(1) == 0)
    def _():
        h_sc[...] = _rms(x_ref[...], g_ref[...]).astype(BF16)

    h = h_sc[...]
    gate = jnp.dot(h, wg_ref[...], preferred_element_type=F32)
    up = jnp.dot(h, wu_ref[...], preferred_element_type=F32)
    o_ref[...] = (gate * jax.nn.sigmoid(gate) * up).astype(o_ref.dtype)


def mm_norm_swiglu(x, g, w_gu):
    m, k = x.shape
    hid = w_gu.shape[1] // 2
    tm = _tile(m, 1024, 8)
    tn = _tile(hid, 512, LANE)
    nj = hid // tn
    return pl.pallas_call(
        _swiglu_kernel,
        grid=(m // tm, nj),
        in_specs=[pl.BlockSpec((tm, k), lambda i, j: (i, 0)),
                  pl.BlockSpec((1, k), lambda i, j: (0, 0)),
                  pl.BlockSpec((k, tn), lambda i, j: (0, j)),
                  pl.BlockSpec((k, tn), lambda i, j: (0, j + nj))],
        out_specs=pl.BlockSpec((tm, tn), lambda i, j: (i, j)),
        out_shape=jax.ShapeDtypeStruct((m, hid), BF16),
        scratch_shapes=[pltpu.VMEM((tm, k), BF16)],
        compiler_params=_params(("parallel", "arbitrary")),
        name="ffn_gate_up",
    )(x, g.reshape(1, k), w_gu, w_gu)


def _mm_res_kernel(a_ref, w_ref, r_ref, o_ref):
    o_ref[...] = r_ref[...] + jnp.dot(a_ref[...], w_ref[...], preferred_element_type=F32)


def mm_res(a, w, res):
    m, k = a.shape
    n = w.shape[1]
    tm = _tile(m, 1024, 8)
    tn = _tile(n, 512, LANE)
    return pl.pallas_call(
        _mm_res_kernel,
        grid=(m // tm, n // tn),
        in_specs=[pl.BlockSpec((tm, k), lambda i, j: (i, 0)),
                  pl.BlockSpec((k, tn), lambda i, j: (0, j)),
                  pl.BlockSpec((tm, tn), lambda i, j: (i, j))],
        out_specs=pl.BlockSpec((tm, tn), lambda i, j: (i, j)),
        out_shape=jax.ShapeDtypeStruct((m, n), F32),
        compiler_params=_params(("parallel", "arbitrary")),
        name="mm_res",
    )(a, w, res)


def _mm_kernel(a_ref, w_ref, o_ref):
    o_ref[...] = jnp.dot(a_ref[...].astype(BF16), w_ref[...], preferred_element_type=F32)


def mm(a, w):
    m, k = a.shape
    n = w.shape[1]
    tm = _tile(m, 512, 8)
    return pl.pallas_call(
        _mm_kernel,
        grid=(m // tm,),
        in_specs=[pl.BlockSpec((tm, k), lambda i: (i, 0)), pl.BlockSpec((k, n), lambda i: (0, 0))],
        out_specs=pl.BlockSpec((tm, n), lambda i: (i, 0)),
        out_shape=jax.ShapeDtypeStruct((m, n), F32),
        compiler_params=_params(("parallel",)),
        name="mm",
    )(a, w)


def _merge_kernel(sub_scale, n_prompt_blocks, x_ref, g_ref, *refs):
    op_refs, os_refs = refs[0:4], refs[4:8]
    sg_ref = refs[8]
    wg_refs, wb_refs = refs[9:13], refs[13:17]
    y_ref, h_sc, o_sc = refs[17:]

    @pl.when(pl.program_id(1) == 0)
    def _():
        h_sc[...] = _rms(x_ref[...], g_ref[...]).astype(BF16)
        from_prompt = pl.program_id(0) < n_prompt_blocks
        for b in range(4):
            o = jnp.where(from_prompt, op_refs[b][...], os_refs[b][...])
            if b == 2:
                for c in range(o.shape[1] // LANE):
                    sl = slice(c * LANE, (c + 1) * LANE)
                    o_sc[b, :, sl] = (_rms(o[:, sl], sg_ref[...]) * sub_scale).astype(BF16)
            else:
                o_sc[b] = o.astype(BF16)

    h = h_sc[...]
    acc = None
    for b in range(4):
        gate = jax.nn.sigmoid(jnp.dot(h, wg_refs[b][...], preferred_element_type=F32))
        yb = jnp.dot(o_sc[b], wb_refs[b][0], preferred_element_type=F32)
        acc = gate * yb if acc is None else acc + gate * yb
    y_ref[...] = acc.astype(y_ref.dtype)


def merge_branches(x, g, outs_p, outs_s, subln_g, sub_scale, w_gate, w_branch):
    m, d = x.shape
    mp, ms = outs_p[0].shape[0], outs_s[0].shape[0]
    tm = _tile(math.gcd(mp, ms), 512, 8)
    npb = mp // tm
    tn = _tile(d, 512, LANE)
    nj = d // tn
    p_spec = pl.BlockSpec((tm, MIX_W), lambda i, j: (jnp.minimum(i, npb - 1), 0))
    s_spec = pl.BlockSpec((tm, MIX_W), lambda i, j: (jnp.maximum(i - npb, 0), 0))
    in_specs = [pl.BlockSpec((tm, d), lambda i, j: (i, 0)),
                pl.BlockSpec((1, d), lambda i, j: (0, 0))] + [p_spec] * 4 + [s_spec] * 4
    in_specs.append(pl.BlockSpec((1, LANE), lambda i, j: (0, 0)))
    for b in range(4):
        in_specs.append(pl.BlockSpec((d, tn), functools.partial(lambda i, j, b: (0, b * nj + j), b=b)))
    for b in range(4):
        in_specs.append(pl.BlockSpec((1, MIX_W, tn), functools.partial(lambda i, j, b: (b, 0, j), b=b)))
    return pl.pallas_call(
        functools.partial(_merge_kernel, sub_scale, npb),
        grid=(m // tm, nj),
        in_specs=in_specs,
        out_specs=pl.BlockSpec((tm, tn), lambda i, j: (i, j)),
        out_shape=jax.ShapeDtypeStruct((m, d), BF16),
        scratch_shapes=[pltpu.VMEM((tm, d), BF16), pltpu.VMEM((4, tm, MIX_W), BF16)],
        compiler_params=_params(("parallel", "arbitrary")),
        name="merge_branches",
    )(x, g.reshape(1, d), *outs_p, *outs_s, subln_g.reshape(1, LANE),
      w_gate, w_gate, w_gate, w_gate, w_branch, w_branch, w_branch, w_branch)


Z_CQ, Z_CKV, Z_SLAB, Z_SLAB_W = 0, 384, 512, 2688
Z_LAST = Z_SLAB + Z_SLAB_W
Z_WIDTH = Z_LAST + LANE
GATE_LANE0 = MLA_ROPE
SLAB_GROUPS = (("q_moba", 512, (1,) * 8), ("moba_kv", 256, (1, 1, 0, 0)),
               ("q_diff", 512, (1,) * 8), ("diff_kv", 512, (1, 1, 1, 1, 0, 0, 0, 0)),
               ("q_nsa", 512, (1,) * 8), ("nsa_kv", 256, (0, 0, 1, 0)), ("nsa_win", 128, (1, 0)))
SLAB_ENABLE = sum((g[2] for g in SLAB_GROUPS), ())
ROW_OUTS = (("qa", 1024), ("qpe", 256), ("ckv", 128), ("last", 128), ("q_moba", 512), ("moba_kv", 256),
            ("q_diff", 512), ("diff_kv", 512), ("q_nsa", 512), ("nsa_kv", 256), ("nsa_win", 128))
ROW_OUTS_T = (("last_t", 128), ("moba_kv_t", 256), ("nsa_kv_t", 256), ("nsa_win_t", 128))


def _rope_apply(y, c, s1, s2, half):
    return y * c + pltpu.roll(y, LANE - half, 1) * s1 + pltpu.roll(y, half, 1) * s2


def _rows_kernel(z_ref, cqg_ref, ckvg_ref, wuq_ref, b768_ref, qg_ref, wqa_ref, slabg_ref,
                 c64_ref, s64a_ref, s64b_ref, cm_ref, sma_ref, smb_ref,
                 qa_ref, qpe_ref, ckv_ref, last_ref, qmoba_ref, mkv_ref, qdiff_ref, dkv_ref,
                 qnsa_ref, nkv_ref, nwin_ref, lastt_ref, mkvt_ref, nkvt_ref, nwint_ref):
    mla_scale = MLA_QK ** -0.5
    cq = _rms(z_ref[:, Z_CQ:Z_CQ + 384], cqg_ref[...])
    q = jnp.dot(cq.astype(BF16), wuq_ref[...], preferred_element_type=F32)
    ss = _dot_x(q * q, b768_ref[...])
    q = q * lax.rsqrt(ss * (1.0 / MLA_QK) + EPS) * qg_ref[...]
    for c in range(2):
        sl = slice(c * LANE, (c + 1) * LANE)
        qr = _rope_apply(q[:, sl], cm_ref[...], sma_ref[...], smb_ref[...], MLA_ROPE // 2)
        qpe_ref[:, sl] = qr * mla_scale
    qa_ref[...] = jnp.dot(q[:, 256:].astype(BF16), wqa_ref[...], preferred_element_type=F32) * mla_scale
    ckv_ref[...] = _rms(z_ref[:, Z_CKV:Z_CKV + LANE], ckvg_ref[...])
    lane = lax.broadcasted_iota(jnp.int32, (1, LANE), 1)
    zl = z_ref[:, Z_LAST:Z_LAST + LANE]
    last = jnp.where(lane < MLA_ROPE, zl, jax.nn.sigmoid(zl))
    last_ref[...] = last
    lastt_ref[...] = last.T
    ri = lax.broadcasted_iota(jnp.int32, (LANE, LANE), 0) // HEAD_DIM
    ci = lax.broadcasted_iota(jnp.int32, (LANE, LANE), 1) // HEAD_DIM
    b128 = (ri == ci).astype(BF16)
    refs = {"q_moba": (qmoba_ref, None), "moba_kv": (mkv_ref, mkvt_ref), "q_diff": (qdiff_ref, None),
            "diff_kv": (dkv_ref, None), "q_nsa": (qnsa_ref, None), "nsa_kv": (nkv_ref, nkvt_ref),
            "nsa_win": (nwin_ref, nwint_ref)}
    c = 0
    for name, width, _ in SLAB_GROUPS:
        ref, ref_t = refs[name]
        for k in range(width // LANE):
            x = z_ref[:, Z_SLAB + c * LANE:Z_SLAB + (c + 1) * LANE]
            e0, e1 = SLAB_ENABLE[2 * c], SLAB_ENABLE[2 * c + 1]
            if e0 or e1:
                ss = _dot_x(x * x, b128)
                y = x * lax.rsqrt(ss * (1.0 / HEAD_DIM) + EPS) * slabg_ref[:, c * LANE:(c + 1) * LANE]
                y = _rope_apply(y, c64_ref[...], s64a_ref[...], s64b_ref[...], ROT_DIM // 2)
                if not (e0 and e1):
                    y = jnp.where((lane < HEAD_DIM) if e0 else (lane >= HEAD_DIM), y, x)
            else:
                y = x
            if name.startswith("q_"):
                y = y * (HEAD_DIM ** -0.5)
            ref[:, k * LANE:(k + 1) * LANE] = y
            if ref_t is not None:
                ref_t[k * LANE:(k + 1) * LANE, :] = y.T
            c += 1


def token_rows(z, lw, tabs):
    m = z.shape[0]
    tm = _tile(m, 256, LANE)
    row = lambda w: pl.BlockSpec((tm, w), lambda i: (i, 0))
    col = lambda w: pl.BlockSpec((w, tm), lambda i: (0, i))
    full = lambda a: pl.BlockSpec(a.shape, lambda i: (0,) * a.ndim)
    consts = (lw["cq_g"], lw["ckv_g"], lw["wuq"], lw["b768"], lw["q_g768"], lw["wqa"], lw["slab_g"])
    outs = pl.pallas_call(
        _rows_kernel,
        grid=(m // tm,),
        in_specs=[row(Z_WIDTH)] + [full(a) for a in consts] + [row(LANE)] * 6,
        out_specs=[row(w) for _, w in ROW_OUTS] + [col(w) for _, w in ROW_OUTS_T],
        out_shape=([jax.ShapeDtypeStruct((m, w), F32) for _, w in ROW_OUTS]
                   + [jax.ShapeDtypeStruct((w, m), F32) for _, w in ROW_OUTS_T]),
        compiler_params=_params(("parallel",)),
        name="token_rows",
    )(z, *consts, *tabs)
    return dict(zip([n for n, _ in ROW_OUTS + ROW_OUTS_T], outs))


def _flash_init(m_sc, l_sc, acc_sc):
    m_sc[...] = jnp.full(m_sc.shape, NEG, F32)
    l_sc[...] = jnp.zeros(l_sc.shape, F32)
    acc_sc[...] = jnp.zeros(acc_sc.shape, F32)


def _masked(x, mask, fill):
    if mask.shape[0] == x.shape[0]:
        return jnp.where(mask, x, fill)
    nq, t = mask.shape
    return jnp.where(mask[None], x.reshape(x.shape[0] // nq, nq, t), fill).reshape(x.shape)


def _flash_update(s, mask, pv, m_sc, l_sc, acc_sc, idx=None):
    at = (lambda r: r.at[idx]) if idx is not None else (lambda r: r)
    if mask is not None:
        s = _masked(s, mask, NEG)
    m_prev = at(m_sc)[...]
    m_new = jnp.maximum(m_prev, jnp.max(s, axis=-1, keepdims=True))
    alpha = jnp.exp(m_prev - m_new)
    p = jnp.exp(s - m_new)
    if mask is not None:
        p = _masked(p, mask, 0.0)
    at(l_sc)[...] = alpha * at(l_sc)[...] + jnp.sum(p, axis=-1, keepdims=True)
    at(acc_sc)[...] = alpha * at(acc_sc)[...] + pv(p)
    at(m_sc)[...] = m_new


def _flash_out(l_sc, acc_sc, idx=None):
    at = (lambda r: r.at[idx]) if idx is not None else (lambda r: r)
    return at(acc_sc)[...] / jnp.maximum(at(l_sc)[...], 1e-30)


def _row_tok(nrows, nq):
    return lax.broadcasted_iota(jnp.int32, (nrows, 1), 0) % nq


def _cat(vals, axis):
    return vals[0] if len(vals) == 1 else jnp.concatenate(vals, axis=axis)


def _ld(ref):
    return ref[(0,) * (len(ref.shape) - 2)]


def _lane():
    return lax.broadcasted_iota(jnp.int32, (1, LANE), 1)


def _rows_from_heads(q, place):
    lane = _lane()
    out = []
    for h in range(N_HEADS):
        x = q[:, (h // 2) * LANE:(h // 2 + 1) * LANE]
        if h % 2 != place[h]:
            x = pltpu.roll(x, HEAD_DIM, 1)
        out.append(jnp.where((lane < HEAD_DIM) if place[h] == 0 else (lane >= HEAD_DIM), x, 0.0))
    return jnp.concatenate(out, axis=0)


def _heads_from_rows(o, src, nq):
    lane = _lane()
    chunks = []
    for c in range(N_HEADS // 2):
        lo = o[(2 * c) * nq:(2 * c + 1) * nq]
        hi = o[(2 * c + 1) * nq:(2 * c + 2) * nq]
        if src[2 * c] == 1:
            lo = pltpu.roll(lo, HEAD_DIM, 1)
        if src[2 * c + 1] == 0:
            hi = pltpu.roll(hi, HEAD_DIM, 1)
        chunks.append(jnp.where(lane < HEAD_DIM, lo, hi))
    return jnp.concatenate(chunks, axis=1)


MOBA_PLACE = tuple(h // (N_HEADS // 2) for h in range(N_HEADS))
NSA_PLACE = (0,) * N_HEADS
NSA_SRC = (1,) * N_HEADS


class _Geom:
    def __init__(self, sample, batch, seq, row0, nq, tok_per_ref, n_ref, n_chunk, past_len):
        self.sample = sample
        self.batch = batch
        self.seq = seq
        self.row0 = row0
        self.nq = nq
        self.n_qb = seq // nq
        self.tok_per_ref = tok_per_ref
        self.n_ref = n_ref
        self.tc = tok_per_ref * n_ref
        self.n_chunk = n_chunk
        self.past_len = past_len

    def q_base(self, qb):
        return self.past_len if self.sample else qb * self.nq

    def q_block(self, i, qb):
        return (self.row0 + i * self.seq) // self.nq + qb

    def out_block(self, i, qb):
        return i * self.n_qb + qb

    def chunk_live(self, qb, c):
        return c * self.tc <= qb * self.nq + self.nq - 1

    def chunk_block(self, i, qb, c):
        last = jnp.minimum((qb * self.nq + self.nq - 1) // self.tc, self.n_chunk - 1)
        return (self.row0 + i * self.seq) // self.tc + jnp.minimum(c, last)

    def new_block(self, i):
        return (self.row0 + i * self.seq) // LANE

    def new_off(self, i):
        return (self.row0 + i * self.seq) % LANE

    def new_kpos(self, i, live):
        rel = _lane() - self.new_off(i)
        return jnp.where((rel >= 0) & (rel < self.seq) & live, self.past_len + rel, FAR)


POOL_SPEC = pl.BlockSpec(memory_space=pl.ANY)


def _page_scratch(geom, rows):
    return [pltpu.VMEM((2, geom.n_ref, rows, LANE), F32), pltpu.SemaphoreType.DMA((2,))]


def _paged_fetch(geom, li, pt_ref, pool_ref, buf, sem, row_lo, c_axis):
    n, nch = geom.n_ref, geom.n_chunk
    rows = buf.shape[2]
    step = pl.program_id(0) * nch + pl.program_id(c_axis)
    slot = step % 2

    def copies(st, sl, lookup):
        ii, cc = st // nch, st % nch
        out = []
        for j in range(n):
            page = pt_ref[ii, cc * n + j] if lookup else 0
            out.append(pltpu.make_async_copy(pool_ref.at[li, page, pl.ds(row_lo, rows), :],
                                             buf.at[sl, j], sem.at[sl]))
        return out

    @pl.when(step == 0)
    def _():
        for cp in copies(step, slot, True):
            cp.start()

    @pl.when(step + 1 < geom.batch * nch)
    def _():
        for cp in copies(step + 1, 1 - slot, True):
            cp.start()

    for cp in copies(step, slot, False):
        cp.wait()
    return [buf.at[slot, j] for j in range(n)]


def _mask_tail(s, kpos, qpos, start):
    if start == 0:
        return jnp.where(kpos <= qpos, s, NEG)
    tail = jnp.where(kpos[:, start:] <= qpos, s[:, start:], NEG)
    return jnp.concatenate([s[:, :start], tail], axis=1)


def _mla_kernel(geom, li, pt_ref, qa_ref, qpe_ref, *refs):
    if geom.sample:
        (ckv_pool, kpe_pool, ckvn_ref, kpen_ref, cosn_ref, sinn_ref, cos_ref, sin_ref, wukt_ref, gpe_ref,
         o_ref, m_sc, l_sc, acc_sc, ckv_buf, ckv_sem, kpe_buf, kpe_sem) = refs
        ckv_refs = _paged_fetch(geom, li, pt_ref, ckv_pool, ckv_buf, ckv_sem, 0, 2) + [ckvn_ref]
        kpe_refs = _paged_fetch(geom, li, pt_ref, kpe_pool, kpe_buf, kpe_sem, 0, 2) + [kpen_ref]
    else:
        ckv_ref, kpe_ref, cos_ref, sin_ref, wukt_ref, gpe_ref, o_ref, m_sc, l_sc, acc_sc = refs
        ckv_refs, kpe_refs = [ckv_ref], [kpe_ref]
    nq, tc = geom.nq, geom.tc
    nrows = MLA_HEADS * nq
    i, qb, c = pl.program_id(0), pl.program_id(1), pl.program_id(2)
    qpos = geom.q_base(qb) + _row_tok(nrows, nq)
    lane = _lane()

    def process():
        qa = jnp.concatenate([qa_ref[:, h * LANE:(h + 1) * LANE] for h in range(MLA_HEADS)], axis=0)
        qpe = []
        for h in range(MLA_HEADS):
            x = qpe_ref[:, (h // 4) * LANE:(h // 4 + 1) * LANE]
            if h % 4:
                x = pltpu.roll(x, LANE - (h % 4) * MLA_ROPE, 1)
            qpe.append(jnp.where(lane < MLA_ROPE, x, 0.0))
        qpe = jnp.concatenate(qpe, axis=0)
        ckv = _cat([_ld(r) for r in ckv_refs], 0)
        kpet = _cat([_ld(r) for r in kpe_refs], 1)
        cos, sin = cos_ref[...], sin_ref[...]
        kpos = c * tc + lax.broadcasted_iota(jnp.int32, (1, tc), 1)
        if geom.sample:
            cos = jnp.concatenate([cos, cosn_ref[0]], axis=1)
            sin = jnp.concatenate([sin, sinn_ref[0]], axis=1)
            kpos = jnp.concatenate([kpos, geom.new_kpos(i, c == geom.n_chunk - 1)], axis=1)
        t = ckv.shape[0]
        ckv_b = ckv.astype(BF16)
        knt = lax.dot_general(wukt_ref[...], ckv_b, (((1,), (1,)), ((), ())),
                              preferred_element_type=F32)
        ss = jnp.sum((knt * knt).reshape(MLA_HEADS, MLA_NOPE, t), axis=1)
        pe2 = jnp.sum(kpet * kpet, axis=0, keepdims=True)
        rt = lax.rsqrt((ss + pe2) * (1.0 / MLA_QK) + EPS)
        kg = kpet * gpe_ref[...]
        x1, x2 = kg[:MLA_ROPE // 2], kg[MLA_ROPE // 2:]
        kr = jnp.concatenate([x1 * cos - x2 * sin, x2 * cos + x1 * sin,
                              jnp.zeros((LANE - MLA_ROPE, t), F32)], axis=0)
        s = _dot_nt(qa, ckv_b) + _dot(qpe, kr)
        s = jnp.concatenate([s[h * nq:(h + 1) * nq] * rt[h:h + 1] for h in range(MLA_HEADS)], axis=0)
        s = _mask_tail(s, kpos, qpos, tc if geom.sample else 0)
        _flash_update(s, None, lambda p: _dot(p, ckv_b), m_sc, l_sc, acc_sc)

    @pl.when(c == 0)
    def _():
        _flash_init(m_sc, l_sc, acc_sc)

    if geom.sample:
        process()
    else:
        pl.when(geom.chunk_live(qb, c))(process)

    @pl.when(c == geom.n_chunk - 1)
    def _():
        lat = _flash_out(l_sc, acc_sc)
        for h in range(MLA_HEADS):
            o_ref[:, h * LANE:(h + 1) * LANE] = lat[h * nq:(h + 1) * nq]


def mla_attention(geom, li, pt, r, pools, tabs, lw):
    n, tc, tpr, nq = geom.n_ref, geom.tc, geom.tok_per_ref, geom.nq
    cos_t, sin_t, cos_n, sin_n = tabs
    in_specs = [pl.BlockSpec((nq, 8 * LANE), lambda i, q, c, pt: (geom.q_block(i, q), 0)),
                pl.BlockSpec((nq, 2 * LANE), lambda i, q, c, pt: (geom.q_block(i, q), 0))]
    args = [r["qa"], r["qpe"]]
    if geom.sample:
        in_specs += [POOL_SPEC, POOL_SPEC]
        args += [pools[0], pools[1]]
        noff = LANE // geom.seq
        in_specs += [pl.BlockSpec((LANE, LANE), lambda i, q, c, pt: (geom.new_block(i), 0)),
                     pl.BlockSpec((MLA_ROPE, LANE), lambda i, q, c, pt: (0, geom.new_block(i))),
                     pl.BlockSpec((1, MLA_ROPE // 2, LANE), lambda i, q, c, pt: (geom.new_off(i) // geom.seq, 0, 0)),
                     pl.BlockSpec((1, MLA_ROPE // 2, LANE), lambda i, q, c, pt: (geom.new_off(i) // geom.seq, 0, 0))]
        args += [r["ckv"], r["last_t"], cos_n, sin_n]
        assert cos_n.shape[0] == noff
        tab_spec = pl.BlockSpec((MLA_ROPE // 2, tc), lambda i, q, c, pt: (0, c))
    else:
        in_specs += [pl.BlockSpec((tc, LANE), lambda i, q, c, pt: (geom.chunk_block(i, q, c), 0)),
                     pl.BlockSpec((MLA_ROPE, tc), lambda i, q, c, pt: (0, geom.chunk_block(i, q, c)))]
        args += [r["ckv"], r["last_t"]]
        tab_spec = pl.BlockSpec((MLA_ROPE // 2, tc), lambda i, q, c, pt: (0, geom.chunk_block(0, q, c)))
    in_specs += [tab_spec, tab_spec,
                 pl.BlockSpec(lw["wukt"].shape, lambda i, q, c, pt: (0, 0)),
                 pl.BlockSpec((MLA_ROPE, 1), lambda i, q, c, pt: (0, 0))]
    args += [cos_t, sin_t, lw["wukt"], lw["k_gpe"]]
    nrows = MLA_HEADS * nq
    scratch = [pltpu.VMEM((nrows, 1), F32), pltpu.VMEM((nrows, 1), F32), pltpu.VMEM((nrows, LANE), F32)]
    if geom.sample:
        scratch += _page_scratch(geom, tpr) + _page_scratch(geom, MLA_ROPE)
    return pl.pallas_call(
        functools.partial(_mla_kernel, geom, li),
        grid_spec=pltpu.PrefetchScalarGridSpec(
            num_scalar_prefetch=1, grid=(geom.batch, geom.n_qb, geom.n_chunk), in_specs=in_specs,
            out_specs=pl.BlockSpec((nq, 8 * LANE), lambda i, q, c, pt: (geom.out_block(i, q), 0)),
            scratch_shapes=scratch),
        out_shape=jax.ShapeDtypeStruct((geom.batch * geom.seq, 8 * LANE), F32),
        compiler_params=_params(("arbitrary", "arbitrary", "arbitrary")),
        name="mla_sample" if geom.sample else "mla_prompt",
    )(pt, *args)


def _diff_kernel(geom, li, lam_init, pt_ref, q_ref, *refs):
    if geom.sample:
        pool, new_ref, lam_ref, o_ref, m_sc, l_sc, acc_sc, buf, sem = refs
        kv_refs = _paged_fetch(geom, li, pt_ref, pool, buf, sem, 0, 2) + [new_ref]
    else:
        kv_ref, lam_ref, o_ref, m_sc, l_sc, acc_sc = refs
        kv_refs = [kv_ref]
    nq, tc, tpr = geom.nq, geom.tc, geom.tok_per_ref
    nrows = 4 * nq
    i, qb, c = pl.program_id(0), pl.program_id(1), pl.program_id(2)
    qpos = geom.q_base(qb) + _row_tok(nrows, nq)
    lane = _lane()

    def rows_of(ref, k):
        rows = pl.ds(k, ref.shape[-2] // 4, stride=4)
        return ref[rows, :] if len(ref.shape) == 2 else ref[0, 0, rows, :]

    def process():
        kpos = c * tc + lax.broadcasted_iota(jnp.int32, (1, tc), 1)
        if geom.sample:
            kpos = jnp.concatenate([kpos, geom.new_kpos(i, c == geom.n_chunk - 1)], axis=1)
        for g in range(2):
            qg = []
            for comp in range(2):
                for rr in range(2):
                    x = q_ref[:, (2 * g + rr) * LANE:(2 * g + rr + 1) * LANE]
                    qg.append(jnp.where((lane < HEAD_DIM) if comp == 0 else (lane >= HEAD_DIM), x, 0.0))
            qg = jnp.concatenate(qg, axis=0)
            k_b = _cat([rows_of(r, g) for r in kv_refs], 0).astype(BF16)
            v_b = _cat([rows_of(r, 2 + g) for r in kv_refs], 0).astype(BF16)
            s = _mask_tail(_dot_nt(qg, k_b), kpos, qpos, tc if geom.sample else 0)
            _flash_update(s, None, lambda p: _dot(p, v_b), m_sc, l_sc, acc_sc, idx=g)

    @pl.when(c == 0)
    def _():
        _flash_init(m_sc, l_sc, acc_sc)

    if geom.sample:
        process()
    else:
        pl.when(geom.chunk_live(qb, c))(process)

    @pl.when(c == geom.n_chunk - 1)
    def _():
        lv = lam_ref[...]
        lam = (jnp.exp(jnp.sum(lv[0:1] * lv[1:2], axis=-1, keepdims=True))
               - jnp.exp(jnp.sum(lv[2:3] * lv[3:4], axis=-1, keepdims=True)) + lam_init)
        for g in range(2):
            o = _flash_out(l_sc, acc_sc, idx=g)
            o = o[:2 * nq] - lam * o[2 * nq:]
            for rr in range(2):
                o_ref[:, (2 * g + rr) * LANE:(2 * g + rr + 1) * LANE] = o[rr * nq:(rr + 1) * nq]


def diff_attention(geom, li, pt, r, diff_rows, pool, lam_vecs):
    n, tc, tpr, nq = geom.n_ref, geom.tc, geom.tok_per_ref, geom.nq
    in_specs = [pl.BlockSpec((nq, MIX_W), lambda i, q, c, pt: (geom.q_block(i, q), 0))]
    args = [r["q_diff"]]
    if geom.sample:
        in_specs.append(POOL_SPEC)
        args.append(pool)
        in_specs.append(pl.BlockSpec((4 * LANE, LANE), lambda i, q, c, pt: (geom.new_block(i), 0)))
    else:
        in_specs.append(pl.BlockSpec((4 * tc, LANE), lambda i, q, c, pt: (geom.chunk_block(i, q, c), 0)))
    args.append(diff_rows)
    in_specs.append(pl.BlockSpec((4, HEAD_DIM), lambda i, q, c, pt: (0, 0)))
    args.append(lam_vecs)
    scratch = [pltpu.VMEM((2, 4 * nq, 1), F32), pltpu.VMEM((2, 4 * nq, 1), F32),
               pltpu.VMEM((2, 4 * nq, LANE), F32)]
    if geom.sample:
        scratch += _page_scratch(geom, 4 * tpr)
    return pl.pallas_call(
        functools.partial(_diff_kernel, geom, li, _lambda_init(li)),
        grid_spec=pltpu.PrefetchScalarGridSpec(
            num_scalar_prefetch=1, grid=(geom.batch, geom.n_qb, geom.n_chunk), in_specs=in_specs,
            out_specs=pl.BlockSpec((nq, MIX_W), lambda i, q, c, pt: (geom.out_block(i, q), 0)),
            scratch_shapes=scratch),
        out_shape=jax.ShapeDtypeStruct((geom.batch * geom.seq, MIX_W), F32),
        compiler_params=_params(("arbitrary", "arbitrary", "arbitrary")),
        name="diff_sample" if geom.sample else "diff_prompt",
    )(pt, *args)


def _moba_kernel(geom, li, nb_cand, pt_ref, q_ref, *refs):
    if geom.sample:
        pool, new_ref, o_ref, m_sc, l_sc, km_sc, o_sc, buf, sem = refs
        kv_refs = _paged_fetch(geom, li, pt_ref, pool, buf, sem, 0, 2)
    else:
        kv_ref, o_ref, m_sc, l_sc, km_sc, o_sc = refs
        kv_refs = [kv_ref]
    nq, tc = geom.nq, geom.tc
    nrows = N_HEADS * nq
    nbc = tc // MOBA_BLOCK
    i, qb, c = pl.program_id(0), pl.program_id(1), pl.program_id(2)
    q_base = geom.q_base(qb)
    qpos = q_base + _row_tok(nrows, nq)
    cur = q_base // MOBA_BLOCK
    lane = _lane()
    q = _rows_from_heads(q_ref[...], MOBA_PLACE)

    def block(kvt, blk, kpos):
        kt, vt = kvt[:LANE], kvt[LANE:]
        s = _dot(q, kt)
        if kpos is not None:
            s = jnp.where(kpos <= qpos, s, NEG)
        mb = jnp.max(s, axis=-1, keepdims=True)
        p = jnp.exp(s - mb)
        here = lane == blk
        m_sc[...] = jnp.where(here, mb, m_sc[...])
        l_sc[...] = jnp.where(here, jnp.sum(p, axis=-1, keepdims=True), l_sc[...])
        km_sc[...] = jnp.where(here, jnp.sum(kt, axis=-1, keepdims=True) * (1.0 / MOBA_BLOCK), km_sc[...])
        o_sc[blk] = _dot_nt(p, vt)

    @pl.when((i == 0) & (qb == 0) & (c == 0))
    def _():
        m_sc[...] = jnp.full(m_sc.shape, NEG, F32)
        l_sc[...] = jnp.zeros(l_sc.shape, F32)
        km_sc[...] = jnp.zeros(km_sc.shape, F32)
        if not geom.sample:
            o_sc[...] = jnp.zeros(o_sc.shape, F32)

    if geom.sample:
        block(new_ref[...], nb_cand, geom.new_kpos(i, True))
        ppb = MOBA_BLOCK // geom.tok_per_ref
        for j in range(nbc):
            block(_cat([_ld(r) for r in kv_refs[j * ppb:(j + 1) * ppb]], 1), c * nbc + j, None)
    else:
        for j in range(nbc):
            blk = c * nbc + j

            @pl.when(blk <= cur)
            def _():
                kvt = kv_refs[0][:, j * MOBA_BLOCK:(j + 1) * MOBA_BLOCK]
                kpos = blk * MOBA_BLOCK + lax.broadcasted_iota(jnp.int32, (1, MOBA_BLOCK), 1)
                block(kvt, blk, kpos)

    @pl.when(c == geom.n_chunk - 1)
    def _():
        gate = _dot_xx(q, km_sc[...])
        past = lane < cur
        rank = jnp.zeros(gate.shape, jnp.int32)
        for mth in range(nb_cand):
            gm = gate[:, mth:mth + 1]
            ahead = (gm > gate) | ((gm == gate) & (mth < lane))
            if not geom.sample:
                ahead = ahead & (mth < cur)
            rank = rank + ahead.astype(jnp.int32)
        sel = (past & (rank < MOBA_TOPK) & (jnp.abs(gate) < jnp.inf)) | (lane == cur)
        mm_ = jnp.where(sel, m_sc[...], NEG)
        mtop = jnp.max(mm_, axis=-1, keepdims=True)
        w = jnp.where(sel, jnp.exp(mm_ - mtop), 0.0)
        den = jnp.sum(w * l_sc[...], axis=-1, keepdims=True)
        acc = jnp.zeros((nrows, LANE), F32)
        for blk in range(nb_cand + 1 if geom.sample else nb_cand):
            acc = acc + w[:, blk:blk + 1] * o_sc[blk]
        o_ref[...] = _heads_from_rows(acc / jnp.maximum(den, 1e-30), MOBA_PLACE, nq)


def moba_attention(geom, li, pt, r, pool):
    n, tc, tpr, nq = geom.n_ref, geom.tc, geom.tok_per_ref, geom.nq
    nrows = N_HEADS * nq
    nb_cand = geom.n_chunk * (tc // MOBA_BLOCK)
    assert nb_cand + 1 <= LANE
    in_specs = [pl.BlockSpec((nq, MIX_W), lambda i, q, c, pt: (geom.q_block(i, q), 0))]
    args = [r["q_moba"]]
    if geom.sample:
        in_specs.append(POOL_SPEC)
        args.append(pool)
        in_specs.append(pl.BlockSpec((2 * LANE, LANE), lambda i, q, c, pt: (0, geom.new_block(i))))
    else:
        in_specs.append(pl.BlockSpec((2 * LANE, tc), lambda i, q, c, pt: (0, geom.chunk_block(i, q, c))))
    args.append(r["moba_kv_t"])
    scratch = [pltpu.VMEM((nrows, LANE), F32), pltpu.VMEM((nrows, LANE), F32),
               pltpu.VMEM((LANE, LANE), F32), pltpu.VMEM((nb_cand + 1, nrows, LANE), F32)]
    if geom.sample:
        scratch += _page_scratch(geom, 2 * LANE)
    return pl.pallas_call(
        functools.partial(_moba_kernel, geom, li, nb_cand),
        grid_spec=pltpu.PrefetchScalarGridSpec(
            num_scalar_prefetch=1, grid=(geom.batch, geom.n_qb, geom.n_chunk), in_specs=in_specs,
            out_specs=pl.BlockSpec((nq, MIX_W), lambda i, q, c, pt: (geom.out_block(i, q), 0)),
            scratch_shapes=scratch),
        out_shape=jax.ShapeDtypeStruct((geom.batch * geom.seq, MIX_W), F32),
        compiler_params=_params(("arbitrary", "arbitrary", "arbitrary")),
        name="moba_sample" if geom.sample else "moba_prompt",
    )(pt, *args)


def _gelu_tanh(x):
    return 0.5 * x * (1.0 + jnp.tanh(math.sqrt(2.0 / math.pi) * (x + 0.044715 * x * x * x)))


def _compress_kernel(geom, li, nseg, pt_ref, src_ref, *refs):
    (ptop_ref, pbot_ref, wtop_ref, wbot_ref, b1_ref, w2_ref, kg_ref, c_ref, sa_ref, sb_ref,
     o_ref, x_sc, a_sc, b_sc) = refs[:14]
    if geom.sample:
        buf, sem = refs[14:]
        kv_refs = _paged_fetch(geom, li, pt_ref, src_ref, buf, sem, 0, 1)
    else:
        kv_refs = [src_ref]
    tc, tpr = geom.tc, geom.tok_per_ref
    segc = tc // NSA_CMP_STRIDE
    c = pl.program_id(1)
    for j, r in enumerate(kv_refs):
        x_sc[j * tpr:(j + 1) * tpr, :] = _ld(r).T if geom.sample else _ld(r)
    u = jnp.concatenate([x_sc[pl.ds(r, segc, stride=NSA_CMP_STRIDE), :] for r in range(NSA_CMP_STRIDE)], axis=1)
    row0 = pl.multiple_of(c * segc, 8)
    a_sc[pl.ds(row0, segc), :] = _dot(u + ptop_ref[...], wtop_ref[...])
    b_sc[pl.ds(row0, segc), :] = _dot(u + pbot_ref[...], wbot_ref[...])

    @pl.when(c == geom.n_chunk - 1)
    def _():
        hid = a_sc[...] + pltpu.roll(b_sc[...], nseg - 1, 0) + b1_ref[...]
        kv = _dot(_gelu_tanh(hid), w2_ref[...])
        lane = _lane()
        ri = lax.broadcasted_iota(jnp.int32, (LANE, LANE), 0) // HEAD_DIM
        ci = lax.broadcasted_iota(jnp.int32, (LANE, LANE), 1) // HEAD_DIM
        ss = _dot_x(kv * kv, (ri == ci).astype(BF16))
        y = kv * lax.rsqrt(ss * (1.0 / HEAD_DIM) + EPS) * kg_ref[...]
        y = _rope_apply(y, c_ref[...], sa_ref[...], sb_ref[...], ROT_DIM // 2)
        o_ref[0] = jnp.where(lane < HEAD_DIM, y, kv)


def nsa_compress(geom, li, pt, r, pool, nseg, lw, tabs):
    n, tc, tpr = geom.n_ref, geom.tc, geom.tok_per_ref
    if geom.sample:
        in_specs = [POOL_SPEC]
        args = [pool]
    else:
        in_specs = [pl.BlockSpec((tc, LANE), lambda i, c, pt: ((geom.row0 + i * geom.seq) // tc + c, 0))]
        args = [r["nsa_kv"]]
    consts = (lw["cmp_ptop"], lw["cmp_pbot"], lw["cmp_wtop"], lw["cmp_wbot"], lw["cmp_b1"], lw["cmp_w2"],
              lw["cmp_kg"]) + tuple(tabs)
    in_specs += [pl.BlockSpec(a.shape, lambda i, c, pt: (0, 0)) for a in consts]
    scratch = [pltpu.VMEM((tc, LANE), F32), pltpu.VMEM((nseg, 2 * LANE), F32), pltpu.VMEM((nseg, 2 * LANE), F32)]
    if geom.sample:
        scratch += _page_scratch(geom, LANE)
    return pl.pallas_call(
        functools.partial(_compress_kernel, geom, li, nseg),
        grid_spec=pltpu.PrefetchScalarGridSpec(
            num_scalar_prefetch=1, grid=(geom.batch, geom.n_chunk), in_specs=in_specs,
            out_specs=pl.BlockSpec((1, nseg, LANE), lambda i, c, pt: (i, 0, 0)),
            scratch_shapes=scratch),
        out_shape=jax.ShapeDtypeStruct((geom.batch, nseg, LANE), F32),
        compiler_params=_params(("arbitrary", "arbitrary")),
        name="nsa_compress_sample" if geom.sample else "nsa_compress_prompt",
    )(pt, *args, *consts)


def _win_kernel(geom, pt_ref, q_ref, *refs):
    o_ref = refs[-1]
    kv_refs = refs[:-1]
    nq = geom.nq
    nrows = N_HEADS * nq
    i, qb = pl.program_id(0), pl.program_id(1)
    q_base = geom.q_base(qb)
    qpos = q_base + _row_tok(nrows, nq)
    kvt = jnp.concatenate([_ld(r) for r in kv_refs], axis=1)
    if geom.sample:
        kpos = jnp.concatenate([geom.past_len - NSA_WINDOW + lax.broadcasted_iota(jnp.int32, (1, NSA_WINDOW), 1),
                                geom.new_kpos(i, True)], axis=1)
    else:
        kpos = q_base - NSA_WINDOW + lax.broadcasted_iota(jnp.int32, (1, kvt.shape[1]), 1)
    d = qpos - kpos
    mask = (d >= 0) & (d < NSA_WINDOW) & (kpos >= 0)
    s = jnp.where(mask, _dot(_rows_from_heads(q_ref[...], NSA_PLACE), kvt), NEG)
    p = jnp.where(mask, jnp.exp(s - jnp.max(s, axis=-1, keepdims=True)), 0.0)
    o = _dot_nt(p, kvt)
    o_ref[0, 0] = o / jnp.maximum(jnp.sum(p, axis=-1, keepdims=True), 1e-30)


def nsa_window(geom, li, pt, r, win_state):
    nq = geom.nq
    nrows = N_HEADS * nq
    in_specs = [pl.BlockSpec((nq, MIX_W), lambda i, q, pt: (geom.q_block(i, q), 0))]
    if geom.sample:
        in_specs += [pl.BlockSpec((1, 1, LANE, NSA_WINDOW), lambda i, q, pt: (li, i, 0, 0)),
                     pl.BlockSpec((LANE, LANE), lambda i, q, pt: (0, geom.new_block(i)))]
        args = [win_state, r["nsa_win_t"]]
    else:
        nback = NSA_WINDOW // nq
        in_specs += [pl.BlockSpec((LANE, nq),
                                  functools.partial(lambda i, q, pt, j: (0, geom.q_block(i, 0) + jnp.maximum(q - nback + j, 0)),
                                                    j=j))
                     for j in range(nback + 1)]
        args = [r["nsa_win_t"]] * (nback + 1)
    return pl.pallas_call(
        functools.partial(_win_kernel, geom),
        grid_spec=pltpu.PrefetchScalarGridSpec(
            num_scalar_prefetch=1, grid=(geom.batch, geom.n_qb), in_specs=in_specs,
            out_specs=pl.BlockSpec((1, 1, nrows, LANE), lambda i, q, pt: (i, q, 0, 0))),
        out_shape=jax.ShapeDtypeStruct((geom.batch, geom.n_qb, nrows, LANE), F32),
        compiler_params=_params(("arbitrary", "arbitrary")),
        name="nsa_window_sample" if geom.sample else "nsa_window_prompt",
    )(pt, r["q_nsa"], *args)


def _nsa_kernel(geom, li, nc, nsb, pt_ref, q_ref, kvc_ref, gate_ref, owin_ref, *refs):
    if geom.sample:
        pool, new_ref, o_ref, m_sc, l_sc, acc_sc, ocmp_sc, sel_sc, buf, sem = refs
        kv_refs = _paged_fetch(geom, li, pt_ref, pool, buf, sem, LANE, 2) + [new_ref]
    else:
        kv_ref, o_ref, m_sc, l_sc, acc_sc, ocmp_sc, sel_sc = refs
        kv_refs = [kv_ref]
    nq, tc = geom.nq, geom.tc
    nrows = N_HEADS * nq
    nsbp = sel_sc.shape[1]
    i, qb, c = pl.program_id(0), pl.program_id(1), pl.program_id(2)
    q_base = geom.q_base(qb)
    qtok = q_base + lax.broadcasted_iota(jnp.int32, (nq, 1), 0)
    q = _rows_from_heads(q_ref[...], NSA_PLACE)

    def select_and_compress():
        kvc = kvc_ref[0]
        ncp = kvc.shape[0]
        ci = lax.broadcasted_iota(jnp.int32, (1, ncp), 1)
        cend = ci * NSA_CMP_STRIDE + (NSA_CMP_LEN - 1)
        mask = ((cend <= qtok) & (ci < nc))[None]
        s = _dot_nt(q, kvc).reshape(N_HEADS, nq, ncp)
        s = jnp.where(mask, s, NEG)
        e = jnp.where(mask, jnp.exp(s - jnp.max(s, axis=-1, keepdims=True)), 0.0)
        p = e / jnp.maximum(jnp.sum(e, axis=-1, keepdims=True), 1e-30)
        ocmp_sc[...] = _dot(p.reshape(nrows, ncp), kvc)
        cs = lax.broadcasted_iota(jnp.int32, (ncp, 1), 0) * NSA_CMP_STRIDE
        bs = lax.broadcasted_iota(jnp.int32, (1, nsbp), 1) * NSA_SEL_BLOCK
        covers = ((cs < bs + NSA_SEL_BLOCK) & (cs + NSA_CMP_LEN > bs)).astype(BF16)
        imp = _dot_x(jnp.sum(p, axis=0), covers)
        blk = lax.broadcasted_iota(jnp.int32, (1, nsbp), 1)
        curb = qtok // NSA_SEL_BLOCK
        forced = (blk == curb) | (blk == 0)
        score = jnp.where(forced, SEL_FORCE, jnp.where(blk <= curb, imp, -jnp.inf))
        rank = jnp.zeros(score.shape, jnp.int32)
        for mth in range(nsb):
            sm = score[:, mth:mth + 1]
            rank = rank + ((sm > score) | ((sm == score) & (mth < blk))).astype(jnp.int32)
        sel_sc[...] = ((rank < NSA_TOPN) & (score > -jnp.inf)).astype(F32)

    def process():
        kvt = _cat([_ld(r) for r in kv_refs], 1)
        kpos = c * tc + lax.broadcasted_iota(jnp.int32, (1, tc), 1)
        if geom.sample:
            kpos = jnp.concatenate([kpos, geom.new_kpos(i, c == geom.n_chunk - 1)], axis=1)
        expand = (lax.broadcasted_iota(jnp.int32, (nsbp, 1), 0) == kpos // NSA_SEL_BLOCK).astype(BF16)
        picked = jnp.dot(sel_sc[...].astype(BF16), expand, preferred_element_type=F32) > 0.5
        mask = picked & (kpos <= qtok)
        _flash_update(_dot(q, kvt), mask, lambda p: _dot_nt(p, kvt), m_sc, l_sc, acc_sc)

    @pl.when(c == 0)
    def _():
        _flash_init(m_sc, l_sc, acc_sc)
        select_and_compress()

    if geom.sample:
        process()
    else:
        pl.when(geom.chunk_live(qb, c))(process)

    @pl.when(c == geom.n_chunk - 1)
    def _():
        g = gate_ref[...]
        osel = _flash_out(l_sc, acc_sc)
        outs = []
        for h in range(N_HEADS):
            sl = slice(h * nq, (h + 1) * nq)
            g0 = GATE_LANE0 + 3 * h
            outs.append(g[:, g0:g0 + 1] * ocmp_sc[sl] + g[:, g0 + 1:g0 + 2] * osel[sl]
                        + g[:, g0 + 2:g0 + 3] * owin_ref[0, 0, sl])
        o_ref[...] = _heads_from_rows(jnp.concatenate(outs, axis=0), NSA_SRC, nq)


def nsa_attention(geom, li, pt, r, kvc, o_win, pool, nc, nsb):
    n, tc, tpr, nq = geom.n_ref, geom.tc, geom.tok_per_ref, geom.nq
    nrows = N_HEADS * nq
    nsbp = -(-nsb // LANE) * LANE
    in_specs = [pl.BlockSpec((nq, MIX_W), lambda i, q, c, pt: (geom.q_block(i, q), 0)),
                pl.BlockSpec((1,) + kvc.shape[1:], lambda i, q, c, pt: (i, 0, 0)),
                pl.BlockSpec((nq, LANE), lambda i, q, c, pt: (geom.q_block(i, q), 0)),
                pl.BlockSpec((1, 1, nrows, LANE), lambda i, q, c, pt: (i, q, 0, 0))]
    args = [r["q_nsa"], kvc, r["last"], o_win]
    if geom.sample:
        in_specs.append(POOL_SPEC)
        args.append(pool)
        in_specs.append(pl.BlockSpec((LANE, LANE), lambda i, q, c, pt: (1, geom.new_block(i))))
    else:
        in_specs.append(pl.BlockSpec((LANE, tc), lambda i, q, c, pt: (1, geom.chunk_block(i, q, c))))
    args.append(r["nsa_kv_t"])
    scratch = [pltpu.VMEM((nrows, 1), F32), pltpu.VMEM((nrows, 1), F32), pltpu.VMEM((nrows, LANE), F32),
               pltpu.VMEM((nrows, LANE), F32), pltpu.VMEM((nq, nsbp), F32)]
    if geom.sample:
        scratch += _page_scratch(geom, LANE)
    return pl.pallas_call(
        functools.partial(_nsa_kernel, geom, li, nc, nsb),
        grid_spec=pltpu.PrefetchScalarGridSpec(
            num_scalar_prefetch=1, grid=(geom.batch, geom.n_qb, geom.n_chunk), in_specs=in_specs,
            out_specs=pl.BlockSpec((nq, MIX_W), lambda i, q, c, pt: (geom.out_block(i, q), 0)),
            scratch_shapes=scratch),
        out_shape=jax.ShapeDtypeStruct((geom.batch * geom.seq, MIX_W), F32),
        compiler_params=_params(("arbitrary", "arbitrary", "arbitrary")),
        name="nsa_sample" if geom.sample else "nsa_prompt",
    )(pt, *args)


def _rope_tables(pos, half, theta, period, n_rep):
    inv = theta ** (-jnp.arange(half, dtype=F32) / half)
    ang = pos.astype(F32)[:, None] * inv[None, :]
    cos, sin = jnp.cos(ang), jnp.sin(ang)
    t = pos.shape[0]
    ones = jnp.ones((t, period - 2 * half), F32)
    zeros = jnp.zeros((t, period - 2 * half), F32)
    zh = jnp.zeros((t, half), F32)
    c = jnp.concatenate([cos, cos, ones], axis=1)
    sa = jnp.concatenate([-sin, zh, zeros], axis=1)
    sb = jnp.concatenate([zh, sin, zeros], axis=1)
    return tuple(jnp.tile(a, (1, n_rep)) for a in (c, sa, sb))


def _layer_weights(li, p):
    w_in = p["w_in"][li]
    d = w_in.shape[0]
    wa = jnp.concatenate([w_in[:, 0:512], w_in[:, 544:3232], w_in[:, 512:544], w_in[:, 3232:3256],
                          jnp.zeros((d, LANE - 56), F32)], axis=1).astype(BF16)
    lw = {"wa": wa, "wg": w_in[:, 3256:].astype(BF16)}
    lw["cq_g"] = p["mla_cq_g"][li].reshape(1, -1)
    lw["ckv_g"] = p["mla_ckv_g"][li].reshape(1, -1)
    wuq = p["mla_w_uq"][li]
    lw["wuq"] = jnp.concatenate([wuq[:, :, :MLA_ROPE].reshape(384, -1),
                                 wuq[:, :, MLA_ROPE:].reshape(384, -1)], axis=1).astype(BF16)
    head = jnp.concatenate([jnp.arange(256) // MLA_ROPE, jnp.arange(512) // MLA_NOPE])
    lw["b768"] = (head[:, None] == head[None, :]).astype(BF16)
    qg = p["mla_qn_g"][li]
    lw["q_g768"] = jnp.concatenate([jnp.tile(qg[:MLA_ROPE], 8), jnp.tile(qg[MLA_ROPE:], 8)]).reshape(1, -1)
    kg = p["mla_kn_g"][li]
    wuk = p["mla_w_uk"][li]
    hh = jnp.arange(MLA_HEADS)
    wqa = jnp.zeros((8, MLA_NOPE, 8, LANE), F32)
    wqa = wqa.at[hh, :, hh, :].set(jnp.transpose(wuk, (1, 2, 0)) * kg[MLA_ROPE:][None, :, None])
    lw["wqa"] = wqa.reshape(8 * MLA_NOPE, 8 * LANE).astype(BF16)
    lw["wukt"] = jnp.transpose(wuk, (1, 2, 0)).reshape(8 * MLA_NOPE, LANE).astype(BF16)
    lw["k_gpe"] = kg[:MLA_ROPE].reshape(MLA_ROPE, 1)
    wuv = jnp.zeros((8, LANE, 8, HEAD_DIM), F32)
    wuv = wuv.at[hh, :, hh, :].set(jnp.transpose(p["mla_w_uv"][li], (1, 0, 2)))
    lw["wuv"] = wuv.reshape(8 * LANE, MIX_W).astype(BF16)
    gains = {"q_moba": p["moba_qn_g"][li], "moba_kv": p["moba_kn_g"][li], "q_diff": p["diff_qn_g"][li],
             "diff_kv": p["diff_kn_g"][li], "q_nsa": p["nsa_qn_g"][li]}
    one = jnp.ones((HEAD_DIM,), F32)
    sg = []
    for name, width, en in SLAB_GROUPS:
        for k, e in enumerate(en):
            if name == "nsa_kv":
                sg.append(p["nsa_kn_g"][li, 1] if e else one)
            elif name == "nsa_win":
                sg.append(p["nsa_kn_g"][li, 2] if e else one)
            else:
                sg.append(gains[name] if e else one)
    lw["slab_g"] = jnp.concatenate(sg).reshape(1, -1)
    w1 = p["nsa_cmp_w1"][li].reshape(2, NSA_CMP_LEN, HEAD_DIM, -1)
    hid = w1.shape[-1]
    wkv = jnp.zeros((NSA_CMP_LEN, 2, HEAD_DIM, 2, hid), F32)
    wkv = wkv.at[:, 0, :, 0, :].set(w1[0]).at[:, 1, :, 1, :].set(w1[1])
    wkv = wkv.reshape(NSA_CMP_LEN * 2 * HEAD_DIM, 2 * hid)
    half = NSA_CMP_STRIDE * 2 * HEAD_DIM
    lw["cmp_wtop"] = wkv[:half].astype(BF16)
    lw["cmp_wbot"] = wkv[half:].astype(BF16)
    pos = jnp.transpose(p["nsa_cmp_pos"][li], (1, 0, 2)).reshape(1, NSA_CMP_LEN * 2 * HEAD_DIM)
    lw["cmp_ptop"] = pos[:, :half]
    lw["cmp_pbot"] = pos[:, half:]
    lw["cmp_b1"] = p["nsa_cmp_b1"][li].reshape(1, 2 * hid)
    w2 = p["nsa_cmp_w2"][li]
    w2b = jnp.zeros((2, hid, 2, HEAD_DIM), F32).at[0, :, 0, :].set(w2[0]).at[1, :, 1, :].set(w2[1])
    lw["cmp_w2"] = w2b.reshape(2 * hid, 2 * HEAD_DIM).astype(BF16)
    lw["cmp_kg"] = jnp.concatenate([p["nsa_kn_g"][li, 0], one]).reshape(1, LANE)
    lw["w_branch"] = p["w_branch"][li].astype(BF16)
    lw["w_out"] = p["w_out"][li].astype(BF16)
    lw["w_gu"] = p["ffn_w_gu"][li].astype(BF16)
    lw["w_down"] = p["ffn_w_down"][li].astype(BF16)
    return lw


def _attention_group(geoms, li, pt, r, diff_rows, pools, nseg, nsb, tabs, lw, lam_vecs):
    lat = mla_attention(geoms["mla"], li, pt, r, pools.get("mla"), tabs["mla"], lw)
    o_mla = mm(lat, lw["wuv"])
    o_moba = moba_attention(geoms["moba"], li, pt, r, pools.get("moba"))
    o_diff = diff_attention(geoms["diff"], li, pt, r, diff_rows, pools.get("diff"), lam_vecs)
    o_win = nsa_window(geoms["nsa"], li, pt, r, pools.get("win"))
    kvc = nsa_compress(geoms["cmp"], li, pt, r, pools.get("nsa"), nseg, lw, tabs["cmp"])
    o_nsa = nsa_attention(geoms["nsa"], li, pt, r, kvc, o_win, pools.get("nsa"), nseg - 1, nsb)
    return o_mla, o_moba, o_diff, o_nsa


def kernel(x_prompt, x_sample, cache_mla_ckv, cache_mla_kpe, cache_moba_kv, cache_diff_kv, cache_nsa_kv, state_nsa_win, page_table, ln_attn_g, w_in, mla_cq_g, mla_ckv_g, mla_w_uq, mla_qn_g, mla_w_uk, mla_w_uv, mla_kn_g, moba_qn_g, moba_kn_g, diff_qn_g, diff_kn_g, diff_lambda, diff_subln_g, nsa_qn_g, nsa_kn_g, nsa_cmp_pos, nsa_cmp_w1, nsa_cmp_b1, nsa_cmp_w2, w_branch, w_out, ln_ffn_g, ffn_w_gu, ffn_w_down):
    p = dict(w_in=w_in, mla_cq_g=mla_cq_g, mla_ckv_g=mla_ckv_g, mla_w_uq=mla_w_uq, mla_qn_g=mla_qn_g,
             mla_w_uk=mla_w_uk, mla_w_uv=mla_w_uv, mla_kn_g=mla_kn_g, moba_qn_g=moba_qn_g,
             moba_kn_g=moba_kn_g, diff_qn_g=diff_qn_g, diff_kn_g=diff_kn_g, nsa_qn_g=nsa_qn_g,
             nsa_kn_g=nsa_kn_g, nsa_cmp_pos=nsa_cmp_pos, nsa_cmp_w1=nsa_cmp_w1, nsa_cmp_b1=nsa_cmp_b1,
             nsa_cmp_w2=nsa_cmp_w2, w_branch=w_branch, w_out=w_out, ffn_w_gu=ffn_w_gu, ffn_w_down=ffn_w_down)
    bp, tp, d = x_prompt.shape
    bs, ts, _ = x_sample.shape
    depth, n_pool, page = cache_mla_ckv.shape[:3]
    n_pages = page_table.shape[1]
    past = n_pages * page
    mp, ms = bp * tp, bs * ts
    assert page == LANE and past % MOBA_BLOCK == 0 and state_nsa_win.shape[2] == NSA_WINDOW
    assert tp % MOBA_BLOCK == 0 and tp >= NSA_WINDOW and LANE % ts == 0 and ts % 8 == 0
    assert mp % LANE == 0 and ms % LANE == 0
    nq_p = min(Q_BLOCK, tp)
    n_ref = min(PAGES_PER_STEP, n_pages)
    assert n_pages % n_ref == 0 and (n_ref * page) % MOBA_BLOCK == 0
    tc_p = min(512, tp)
    geom_p = _Geom(False, bp, tp, 0, nq_p, tc_p, 1, tp // tc_p, 0)
    geom_pw = _Geom(False, bp, tp, 0, nq_p, tp, 1, 1, 0)
    geom_s = _Geom(True, bs, ts, mp, ts, page, n_ref, n_pages // n_ref, past)
    geom_sw = _Geom(True, bs, ts, mp, ts, page, n_pages, 1, past)
    geoms_p = {"mla": geom_p, "diff": geom_p, "nsa": geom_p, "moba": geom_pw, "cmp": geom_pw}
    geoms_s = {"mla": geom_s, "diff": geom_s, "nsa": geom_sw, "moba": geom_sw, "cmp": geom_sw}
    pt_dummy = jnp.zeros((1, 1), jnp.int32)

    fm = lambda a: jnp.transpose(a, (0, 1, 3, 4, 5, 2)).reshape(a.shape[0], a.shape[1], -1, a.shape[2])
    pools_s = {"moba": fm(cache_moba_kv),
               "nsa": fm(cache_nsa_kv),
               "win": fm(state_nsa_win),
               "mla": (cache_mla_ckv, jnp.transpose(cache_mla_kpe, (0, 1, 3, 2))),
               "diff": cache_diff_kv.reshape(depth, n_pool, 4 * page, LANE)}

    pos_all = jnp.concatenate([jnp.tile(jnp.arange(tp), bp), jnp.tile(past + jnp.arange(ts), bs)])
    row_tabs = (_rope_tables(pos_all, ROT_DIM // 2, ROPE_THETA, HEAD_DIM, 2)
                + _rope_tables(pos_all, MLA_ROPE // 2, MLA_THETA, MLA_ROPE, 4))
    inv_m = (MLA_THETA ** (-jnp.arange(MLA_ROPE // 2, dtype=F32) / (MLA_ROPE // 2)))[:, None]
    ang_p = inv_m * jnp.arange(max(tp, past), dtype=F32)[None, :]
    cos_k, sin_k = jnp.cos(ang_p), jnp.sin(ang_p)
    rel = jnp.arange(LANE)[None, :] - (jnp.arange(LANE // ts) * ts)[:, None]
    ang_n = inv_m[None] * (past + rel).astype(F32)[:, None, :]
    mla_tabs_p = (cos_k[:, :tp], sin_k[:, :tp], None, None)
    mla_tabs_s = (cos_k[:, :past], sin_k[:, :past], jnp.cos(ang_n), jnp.sin(ang_n))

    def cmp_tabs(nseg):
        cend = jnp.arange(nseg) * NSA_CMP_STRIDE + NSA_CMP_LEN - 1
        c, sa, sb = _rope_tables(cend, ROT_DIM // 2, ROPE_THETA, HEAD_DIM, 1)
        one = jnp.ones((nseg, HEAD_DIM), F32)
        zero = jnp.zeros((nseg, HEAD_DIM), F32)
        return (jnp.concatenate([c, one], 1), jnp.concatenate([sa, zero], 1), jnp.concatenate([sb, zero], 1))

    nseg_p, nseg_s = tp // NSA_CMP_STRIDE, past // NSA_CMP_STRIDE
    tabs_p = {"mla": mla_tabs_p, "cmp": cmp_tabs(nseg_p)}
    tabs_s = {"mla": mla_tabs_s, "cmp": cmp_tabs(nseg_s)}

    x = jnp.concatenate([x_prompt.reshape(mp, d), x_sample.reshape(ms, d)], axis=0)
    news = {k: [] for k in ("ckv", "kpe", "moba", "diff", "nsa", "win")}
    for li in range(depth):
        lw = _layer_weights(li, p)
        z = mm_norm(x, ln_attn_g[li], lw["wa"])
        r = token_rows(z, lw, row_tabs)
        for k, src in (("ckv", "ckv"), ("moba", "moba_kv"), ("diff", "diff_kv"), ("nsa", "nsa_kv"),
                       ("win", "nsa_win")):
            news[k].append(r[src])
        news["kpe"].append(r["last"][:, :MLA_ROPE])
        diff_rows = r["diff_kv"].reshape(4 * (mp + ms), LANE)
        op = _attention_group(geoms_p, li, pt_dummy, r, diff_rows, {}, nseg_p,
                              -(-tp // NSA_SEL_BLOCK), tabs_p, lw, diff_lambda[li])
        osm = _attention_group(geoms_s, li, page_table, r, diff_rows, pools_s, nseg_s,
                               -(-(past + ts) // NSA_SEL_BLOCK), tabs_s, lw, diff_lambda[li])
        y = merge_branches(x, ln_attn_g[li], op, osm, diff_subln_g[li], 1.0 - _lambda_init(li), lw["wg"],
                           lw["w_branch"])
        x = mm_res(y, lw["w_out"], x)
        act = mm_norm_swiglu(x, ln_ffn_g[li], lw["w_gu"])
        x = mm_res(act, lw["w_down"], x)

    def stack(k, tail):
        ps = jnp.stack([a[:mp].reshape((bp, tp) + tail) for a in news[k]])
        ss = jnp.stack([a[mp:].reshape((bs, ts) + tail) for a in news[k]])
        return ps, ss

    ckv_p, ckv_s = stack("ckv", (LANE,))
    kpe_p, kpe_s = stack("kpe", (MLA_ROPE,))
    moba_p, moba_s = stack("moba", (2, 2, HEAD_DIM))
    diff_p, diff_s = stack("diff", (2, 2, 2 * HEAD_DIM))
    nsa_p, nsa_s = stack("nsa", (4, 1, HEAD_DIM))
    win_p, win_s = stack("win", (2, 1, HEAD_DIM))
    win_p = win_p[:, :, -NSA_WINDOW:]
    win_s = jnp.concatenate([state_nsa_win, win_s], axis=2)[:, :, -NSA_WINDOW:]
    return (x[:mp].reshape(bp, tp, d), x[mp:].reshape(bs, ts, d),
            ckv_p, ckv_s, kpe_p, kpe_s, moba_p, moba_s, diff_p, diff_s, nsa_p, nsa_s, win_p, win_s)
```

```python
import functools
import math

import jax
import jax.numpy as jnp
from jax import lax
from jax.experimental import pallas as pl
from jax.experimental.pallas import tpu as pltpu

F32 = jnp.float32
BF16 = jnp.bfloat16

EPS = 1e-6
NEG = -1e30
FAR = 1 << 28
LANE = 128
VMEM_LIMIT = 56 * 1024 * 1024

HEAD_DIM = 64
ROT_DIM = 16
ROPE_THETA = 500000.0
MLA_HEADS = 8
MLA_NOPE = 64
MLA_ROPE = 32
MLA_QK = MLA_NOPE + MLA_ROPE
MLA_THETA = 10000.0
MOBA_BLOCK = 256
MOBA_TOPK = 3
NSA_CMP_LEN = 32
NSA_CMP_STRIDE = 16
NSA_SEL_BLOCK = 64
NSA_TOPN = 16
NSA_WINDOW = 512
SEL_FORCE = 1e30
N_HEADS = 8
MIX_W = N_HEADS * HEAD_DIM
Q_BLOCK = 128
PAGES_PER_STEP = 32


def _lambda_init(li):
    return 0.8 - 0.6 * math.exp(-0.3 * li)


def _tile(n, target, mult):
    best = None
    d = mult
    while d <= min(n, target):
        if n % d == 0:
            best = d
        d += mult
    return best if best is not None else n


def _params(sem):
    return pltpu.CompilerParams(dimension_semantics=sem, vmem_limit_bytes=VMEM_LIMIT)


def _dot(a, b):
    return jnp.dot(a.astype(BF16), b.astype(BF16), preferred_element_type=F32)


def _dot_nt(a, b):
    return lax.dot_general(a.astype(BF16), b.astype(BF16), (((1,), (1,)), ((), ())),
                           preferred_element_type=F32)


def _split(a):
    hi = a.astype(BF16)
    lo = (a - hi.astype(F32)).astype(BF16)
    return hi, lo


def _dot_x(a, b_exact):
    hi, lo = _split(a)
    b = b_exact.astype(BF16)
    return (jnp.dot(hi, b, preferred_element_type=F32)
            + jnp.dot(lo, b, preferred_element_type=F32))


def _dot_xx(a, b):
    ah, al = _split(a)
    bh, bl = _split(b)
    return (jnp.dot(ah, bh, preferred_element_type=F32)
            + jnp.dot(ah, bl, preferred_element_type=F32)
            + jnp.dot(al, bh, preferred_element_type=F32))


def _rms(x, g):
    r = lax.rsqrt(jnp.mean(x * x, axis=-1, keepdims=True) + EPS)
    return x * r * g


def _mm_norm_kernel(x_ref, g_ref, w_ref, o_ref, h_sc):
    @pl.when(pl.program_id(1) == 0)
    def _():
        h_sc[...] = _rms(x_ref[...], g_ref[...]).astype(BF16)

    o_ref[...] = jnp.dot(h_sc[...], w_ref[...], preferred_element_type=F32)


def mm_norm(x, g, w):
    m, k = x.shape
    n = w.shape[1]
    tm = _tile(m, 512, 8)
    tn = _tile(n, 2048, LANE)
    return pl.pallas_call(
        _mm_norm_kernel,
        grid=(m // tm, n // tn),
        in_specs=[pl.BlockSpec((tm, k), lambda i, j: (i, 0)),
                  pl.BlockSpec((1, k), lambda i, j: (0, 0)),
                  pl.BlockSpec((k, tn), lambda i, j: (0, j))],
        out_specs=pl.BlockSpec((tm, tn), lambda i, j: (i, j)),
        out_shape=jax.ShapeDtypeStruct((m, n), F32),
        scratch_shapes=[pltpu.VMEM((tm, k), BF16)],
        compiler_params=_params(("parallel", "arbitrary")),
        name="mm_norm",
    )(x, g.reshape(1, k), w)


def _swiglu_kernel(x_ref, g_ref, wg_ref, wu_ref, o_ref, h_sc):
    @pl.when(pl.program_id(1) == 0)
    def _():
        h_sc[...] = _rms(x_ref[...], g_ref[...]).astype(BF16)

    h = h_sc[...]
    gate = jnp.dot(h, wg_ref[...], preferred_element_type=F32)
    up = jnp.dot(h, wu_ref[...], preferred_element_type=F32)
    o_ref[...] = (gate * jax.nn.sigmoid(gate) * up).astype(o_ref.dtype)


def mm_norm_swiglu(x, g, w_gu):
    m, k = x.shape
    hid = w_gu.shape[1] // 2
    tm = _tile(m, 1024, 8)
    tn = _tile(hid, 512, LANE)
    nj = hid // tn
    return pl.pallas_call(
        _swiglu_kernel,
        grid=(m // tm, nj),
        in_specs=[pl.BlockSpec((tm, k), lambda i, j: (i, 0)),
                  pl.BlockSpec((1, k), lambda i, j: (0, 0)),
                  pl.BlockSpec((k, tn), lambda i, j: (0, j)),
                  pl.BlockSpec((k, tn), lambda i, j: (0, j + nj))],
        out_specs=pl.BlockSpec((tm, tn), lambda i, j: (i, j)),
        out_shape=jax.ShapeDtypeStruct((m, hid), BF16),
        scratch_shapes=[pltpu.VMEM((tm, k), BF16)],
        compiler_params=_params(("parallel", "arbitrary")),
        name="ffn_gate_up",
    )(x, g.reshape(1, k), w_gu, w_gu)


def _mm_res_kernel(a_ref, w_ref, r_ref, o_ref):
    o_ref[...] = r_ref[...] + jnp.dot(a_ref[...], w_ref[...], preferred_element_type=F32)


def mm_res(a, w, res):
    m, k = a.shape
    n = w.shape[1]
    tm = _tile(m, 1024, 8)
    tn = _tile(n, 512, LANE)
    return pl.pallas_call(
        _mm_res_kernel,
        grid=(m // tm, n // tn),
        in_specs=[pl.BlockSpec((tm, k), lambda i, j: (i, 0)),
                  pl.BlockSpec((k, tn), lambda i, j: (0, j)),
                  pl.BlockSpec((tm, tn), lambda i, j: (i, j))],
        out_specs=pl.BlockSpec((tm, tn), lambda i, j: (i, j)),
        out_shape=jax.ShapeDtypeStruct((m, n), F32),
        compiler_params=_params(("parallel", "arbitrary")),
        name="mm_res",
    )(a, w, res)


def _mm_kernel(a_ref, w_ref, o_ref):
    o_ref[...] = jnp.dot(a_ref[...].astype(BF16), w_ref[...], preferred_element_type=F32)


def mm(a, w):
    m, k = a.shape
    n = w.shape[1]
    tm = _tile(m, 512, 8)
    return pl.pallas_call(
        _mm_kernel,
        grid=(m // tm,),
        in_specs=[pl.BlockSpec((tm, k), lambda i: (i, 0)), pl.BlockSpec((k, n), lambda i: (0, 0))],
        out_specs=pl.BlockSpec((tm, n), lambda i: (i, 0)),
        out_shape=jax.ShapeDtypeStruct((m, n), F32),
        compiler_params=_params(("parallel",)),
        name="mm",
    )(a, w)


def _merge_kernel(sub_scale, n_prompt_blocks, x_ref, g_ref, *refs):
    op_refs, os_refs = refs[0:4], refs[4:8]
    sg_ref = refs[8]
    wg_refs, wb_refs = refs[9:13], refs[13:17]
    y_ref, h_sc, o_sc = refs[17:]

    @pl.when(pl.program_id(1) == 0)
    def _():
        h_sc[...] = _rms(x_ref[...], g_ref[...]).astype(BF16)
        from_prompt = pl.program_id(0) < n_prompt_blocks
        for b in range(4):
            o = jnp.where(from_prompt, op_refs[b][...], os_refs[b][...])
            if b == 2:
                for c in range(o.shape[1] // LANE):
                    sl = slice(c * LANE, (c + 1) * LANE)
                    o_sc[b, :, sl] = (_rms(o[:, sl], sg_ref[...]) * sub_scale).astype(BF16)
            else:
                o_sc[b] = o.astype(BF16)

    h = h_sc[...]
    acc = None
    for b in range(4):
        gate = jax.nn.sigmoid(jnp.dot(h, wg_refs[b][...], preferred_element_type=F32))
        yb = jnp.dot(o_sc[b], wb_refs[b][0], preferred_element_type=F32)
        acc = gate * yb if acc is None else acc + gate * yb
    y_ref[...] = acc.astype(y_ref.dtype)


def merge_branches(x, g, outs_p, outs_s, subln_g, sub_scale, w_gate, w_branch):
    m, d = x.shape
    mp, ms = outs_p[0].shape[0], outs_s[0].shape[0]
    tm = _tile(math.gcd(mp, ms), 512, 8)
    npb = mp // tm
    tn = _tile(d, 512, LANE)
    nj = d // tn
    p_spec = pl.BlockSpec((tm, MIX_W), lambda i, j: (jnp.minimum(i, npb - 1), 0))
    s_spec = pl.BlockSpec((tm, MIX_W), lambda i, j: (jnp.maximum(i - npb, 0), 0))
    in_specs = [pl.BlockSpec((tm, d), lambda i, j: (i, 0)),
                pl.BlockSpec((1, d), lambda i, j: (0, 0))] + [p_spec] * 4 + [s_spec] * 4
    in_specs.append(pl.BlockSpec((1, LANE), lambda i, j: (0, 0)))
    for b in range(4):
        in_specs.append(pl.BlockSpec((d, tn), functools.partial(lambda i, j, b: (0, b * nj + j), b=b)))
    for b in range(4):
        in_specs.append(pl.BlockSpec((1, MIX_W, tn), functools.partial(lambda i, j, b: (b, 0, j), b=b)))
    return pl.pallas_call(
        functools.partial(_merge_kernel, sub_scale, npb),
        grid=(m // tm, nj),
        in_specs=in_specs,
        out_specs=pl.BlockSpec((tm, tn), lambda i, j: (i, j)),
        out_shape=jax.ShapeDtypeStruct((m, d), BF16),
        scratch_shapes=[pltpu.VMEM((tm, d), BF16), pltpu.VMEM((4, tm, MIX_W), BF16)],
        compiler_params=_params(("parallel", "arbitrary")),
        name="merge_branches",
    )(x, g.reshape(1, d), *outs_p, *outs_s, subln_g.reshape(1, LANE),
      w_gate, w_gate, w_gate, w_gate, w_branch, w_branch, w_branch, w_branch)


Z_CQ, Z_CKV, Z_SLAB, Z_SLAB_W = 0, 384, 512, 2688
Z_LAST = Z_SLAB + Z_SLAB_W
Z_WIDTH = Z_LAST + LANE
GATE_LANE0 = MLA_ROPE
SLAB_GROUPS = (("q_moba", 512, (1,) * 8), ("moba_kv", 256, (1, 1, 0, 0)),
               ("q_diff", 512, (1,) * 8), ("diff_kv", 512, (1, 1, 1, 1, 0, 0, 0, 0)),
               ("q_nsa", 512, (1,) * 8), ("nsa_kv", 256, (0, 0, 1, 0)), ("nsa_win", 128, (1, 0)))
SLAB_ENABLE = sum((g[2] for g in SLAB_GROUPS), ())
ROW_OUTS = (("qa", 1024), ("qpe", 256), ("ckv", 128), ("last", 128), ("q_moba", 512), ("moba_kv", 256),
            ("q_diff", 512), ("diff_kv", 512), ("q_nsa", 512), ("nsa_kv", 256), ("nsa_win", 128))
ROW_OUTS_T = (("last_t", 128), ("moba_kv_t", 256), ("nsa_kv_t", 256), ("nsa_win_t", 128))


def _rope_apply(y, c, s1, s2, half):
    return y * c + pltpu.roll(y, LANE - half, 1) * s1 + pltpu.roll(y, half, 1) * s2


def _rows_kernel(z_ref, cqg_ref, ckvg_ref, wuq_ref, b768_ref, qg_ref, wqa_ref, slabg_ref,
                 c64_ref, s64a_ref, s64b_ref, cm_ref, sma_ref, smb_ref,
                 qa_ref, qpe_ref, ckv_ref, last_ref, qmoba_ref, mkv_ref, qdiff_ref, dkv_ref,
                 qnsa_ref, nkv_ref, nwin_ref, lastt_ref, mkvt_ref, nkvt_ref, nwint_ref):
    mla_scale = MLA_QK ** -0.5
    cq = _rms(z_ref[:, Z_CQ:Z_CQ + 384], cqg_ref[...])
    q = jnp.dot(cq.astype(BF16), wuq_ref[...], preferred_element_type=F32)
    ss = _dot_x(q * q, b768_ref[...])
    q = q * lax.rsqrt(ss * (1.0 / MLA_QK) + EPS) * qg_ref[...]
    for c in range(2):
        sl = slice(c * LANE, (c + 1) * LANE)
        qr = _rope_apply(q[:, sl], cm_ref[...], sma_ref[...], smb_ref[...], MLA_ROPE // 2)
        qpe_ref[:, sl] = qr * mla_scale
    qa_ref[...] = jnp.dot(q[:, 256:].astype(BF16), wqa_ref[...], preferred_element_type=F32) * mla_scale
    ckv_ref[...] = _rms(z_ref[:, Z_CKV:Z_CKV + LANE], ckvg_ref[...])
    lane = lax.broadcasted_iota(jnp.int32, (1, LANE), 1)
    zl = z_ref[:, Z_LAST:Z_LAST + LANE]
    last = jnp.where(lane < MLA_ROPE, zl, jax.nn.sigmoid(zl))
    last_ref[...] = last
    lastt_ref[...] = last.T
    ri = lax.broadcasted_iota(jnp.int32, (LANE, LANE), 0) // HEAD_DIM
    ci = lax.broadcasted_iota(jnp.int32, (LANE, LANE), 1) // HEAD_DIM
    b128 = (ri == ci).astype(BF16)
    refs = {"q_moba": (qmoba_ref, None), "moba_kv": (mkv_ref, mkvt_ref), "q_diff": (qdiff_ref, None),
            "diff_kv": (dkv_ref, None), "q_nsa": (qnsa_ref, None), "nsa_kv": (nkv_ref, nkvt_ref),
            "nsa_win": (nwin_ref, nwint_ref)}
    c = 0
    for name, width, _ in SLAB_GROUPS:
        ref, ref_t = refs[name]
        for k in range(width // LANE):
            x = z_ref[:, Z_SLAB + c * LANE:Z_SLAB + (c + 1) * LANE]
            e0, e1 = SLAB_ENABLE[2 * c], SLAB_ENABLE[2 * c + 1]
            if e0 or e1:
                ss = _dot_x(x * x, b128)
                y = x * lax.rsqrt(ss * (1.0 / HEAD_DIM) + EPS) * slabg_ref[:, c * LANE:(c + 1) * LANE]
                y = _rope_apply(y, c64_ref[...], s64a_ref[...], s64b_ref[...], ROT_DIM // 2)
                if not (e0 and e1):
                    y = jnp.where((lane < HEAD_DIM) if e0 else (lane >= HEAD_DIM), y, x)
            else:
                y = x
            if name.startswith("q_"):
                y = y * (HEAD_DIM ** -0.5)
            ref[:, k * LANE:(k + 1) * LANE] = y
            if ref_t is not None:
                ref_t[k * LANE:(k + 1) * LANE, :] = y.T
            c += 1


def token_rows(z, lw, tabs):
    m = z.shape[0]
    tm = _tile(m, 256, LANE)
    row = lambda w: pl.BlockSpec((tm, w), lambda i: (i, 0))
    col = lambda w: pl.BlockSpec((w, tm), lambda i: (0, i))
    full = lambda a: pl.BlockSpec(a.shape, lambda i: (0,) * a.ndim)
    consts = (lw["cq_g"], lw["ckv_g"], lw["wuq"], lw["b768"], lw["q_g768"], lw["wqa"], lw["slab_g"])
    outs = pl.pallas_call(
        _rows_kernel,
        grid=(m // tm,),
        in_specs=[row(Z_WIDTH)] + [full(a) for a in consts] + [row(LANE)] * 6,
        out_specs=[row(w) for _, w in ROW_OUTS] + [col(w) for _, w in ROW_OUTS_T],
        out_shape=([jax.ShapeDtypeStruct((m, w), F32) for _, w in ROW_OUTS]
                   + [jax.ShapeDtypeStruct((w, m), F32) for _, w in ROW_OUTS_T]),
        compiler_params=_params(("parallel",)),
        name="token_rows",
    )(z, *consts, *tabs)
    return dict(zip([n for n, _ in ROW_OUTS + ROW_OUTS_T], outs))


def _flash_init(m_sc, l_sc, acc_sc):
    m_sc[...] = jnp.full(m_sc.shape, NEG, F32)
    l_sc[...] = jnp.zeros(l_sc.shape, F32)
    acc_sc[...] = jnp.zeros(acc_sc.shape, F32)


def _masked(x, mask, fill):
    if mask.shape[0] == x.shape[0]:
        return jnp.where(mask, x, fill)
    nq, t = mask.shape
    return jnp.where(mask[None], x.reshape(x.shape[0] // nq, nq, t), fill).reshape(x.shape)


def _flash_update(s, mask, pv, m_sc, l_sc, acc_sc, idx=None):
    at = (lambda r: r.at[idx]) if idx is not None else (lambda r: r)
    if mask is not None:
        s = _masked(s, mask, NEG)
    m_prev = at(m_sc)[...]
    m_new = jnp.maximum(m_prev, jnp.max(s, axis=-1, keepdims=True))
    alpha = jnp.exp(m_prev - m_new)
    p = jnp.exp(s - m_new)
    if mask is not None:
        p = _masked(p, mask, 0.0)
    at(l_sc)[...] = alpha * at(l_sc)[...] + jnp.sum(p, axis=-1, keepdims=True)
    at(acc_sc)[...] = alpha * at(acc_sc)[...] + pv(p)
    at(m_sc)[...] = m_new


def _flash_out(l_sc, acc_sc, idx=None):
    at = (lambda r: r.at[idx]) if idx is not None else (lambda r: r)
    return at(acc_sc)[...] / jnp.maximum(at(l_sc)[...], 1e-30)


def _row_tok(nrows, nq):
    return lax.broadcasted_iota(jnp.int32, (nrows, 1), 0) % nq


def _cat(vals, axis):
    return vals[0] if len(vals) == 1 else jnp.concatenate(vals, axis=axis)


def _ld(ref):
    return ref[(0,) * (len(ref.shape) - 2)]


def _lane():
    return lax.broadcasted_iota(jnp.int32, (1, LANE), 1)


def _rows_from_heads(q, place):
    lane = _lane()
    out = []
    for h in range(N_HEADS):
        x = q[:, (h // 2) * LANE:(h // 2 + 1) * LANE]
        if h % 2 != place[h]:
            x = pltpu.roll(x, HEAD_DIM, 1)
        out.append(jnp.where((lane < HEAD_DIM) if place[h] == 0 else (lane >= HEAD_DIM), x, 0.0))
    return jnp.concatenate(out, axis=0)


def _heads_from_rows(o, src, nq):
    lane = _lane()
    chunks = []
    for c in range(N_HEADS // 2):
        lo = o[(2 * c) * nq:(2 * c + 1) * nq]
        hi = o[(2 * c + 1) * nq:(2 * c + 2) * nq]
        if src[2 * c] == 1:
            lo = pltpu.roll(lo, HEAD_DIM, 1)
        if src[2 * c + 1] == 0:
            hi = pltpu.roll(hi, HEAD_DIM, 1)
        chunks.append(jnp.where(lane < HEAD_DIM, lo, hi))
    return jnp.concatenate(chunks, axis=1)


MOBA_PLACE = tuple(h // (N_HEADS // 2) for h in range(N_HEADS))
NSA_PLACE = (0,) * N_HEADS
NSA_SRC = (1,) * N_HEADS


class _Geom:
    def __init__(self, sample, batch, seq, row0, nq, tok_per_ref, n_ref, n_chunk, past_len):
        self.sample = sample
        self.batch = batch
        self.seq = seq
        self.row0 = row0
        self.nq = nq
        self.n_qb = seq // nq
        self.tok_per_ref = tok_per_ref
        self.n_ref = n_ref
        self.tc = tok_per_ref * n_ref
        self.n_chunk = n_chunk
        self.past_len = past_len

    def q_base(self, qb):
        return self.past_len if self.sample else qb * self.nq

    def q_block(self, i, qb):
        return (self.row0 + i * self.seq) // self.nq + qb

    def out_block(self, i, qb):
        return i * self.n_qb + qb

    def chunk_live(self, qb, c):
        return c * self.tc <= qb * self.nq + self.nq - 1

    def chunk_block(self, i, qb, c):
        last = jnp.minimum((qb * self.nq + self.nq - 1) // self.tc, self.n_chunk - 1)
        return (self.row0 + i * self.seq) // self.tc + jnp.minimum(c, last)

    def new_block(self, i):
        return (self.row0 + i * self.seq) // LANE

    def new_off(self, i):
        return (self.row0 + i * self.seq) % LANE

    def new_kpos(self, i, live):
        rel = _lane() - self.new_off(i)
        return jnp.where((rel >= 0) & (rel < self.seq) & live, self.past_len + rel, FAR)


POOL_SPEC = pl.BlockSpec(memory_space=pl.ANY)


def _page_scratch(geom, rows):
    return [pltpu.VMEM((2, geom.n_ref, rows, LANE), F32), pltpu.SemaphoreType.DMA((2,))]


def _paged_fetch(geom, li, pt_ref, pool_ref, buf, sem, row_lo, c_axis):
    n, nch = geom.n_ref, geom.n_chunk
    rows = buf.shape[2]
    step = pl.program_id(0) * nch + pl.program_id(c_axis)
    slot = step % 2

    def copies(st, sl, lookup):
        ii, cc = st // nch, st % nch
        out = []
        for j in range(n):
            page = pt_ref[ii, cc * n + j] if lookup else 0
            out.append(pltpu.make_async_copy(pool_ref.at[li, page, pl.ds(row_lo, rows), :],
                                             buf.at[sl, j], sem.at[sl]))
        return out

    @pl.when(step == 0)
    def _():
        for cp in copies(0, 0, True):
            cp.start()

    @pl.when(step + 1 < geom.batch * nch)
    def _():
        for cp in copies(step + 1, 1 - slot, True):
            cp.start()

    for cp in copies(step, slot, False):
        cp.wait()
    return [buf.at[slot, j] for j in range(n)]


def _mask_tail(s, kpos, qpos, start):
    if start == 0:
        return jnp.where(kpos <= qpos, s, NEG)
    tail = jnp.where(kpos[:, start:] <= qpos, s[:, start:], NEG)
    return jnp.concatenate([s[:, :start], tail], axis=1)


def _mla_kernel(geom, li, pt_ref, qa_ref, qpe_ref, *refs):
    if geom.sample:
        (ckv_pool, kpe_pool, ckvn_ref, kpen_ref, cosn_ref, sinn_ref, cos_ref, sin_ref, wukt_ref, gpe_ref,
         o_ref, m_sc, l_sc, acc_sc, ckv_buf, ckv_sem, kpe_buf, kpe_sem) = refs
        ckv_refs = _paged_fetch(geom, li, pt_ref, ckv_pool, ckv_buf, ckv_sem, 0, 2) + [ckvn_ref]
        kpe_refs = _paged_fetch(geom, li, pt_ref, kpe_pool, kpe_buf, kpe_sem, 0, 2) + [kpen_ref]
    else:
        ckv_ref, kpe_ref, cos_ref, sin_ref, wukt_ref, gpe_ref, o_ref, m_sc, l_sc, acc_sc = refs
        ckv_refs, kpe_refs = [ckv_ref], [kpe_ref]
    nq, tc = geom.nq, geom.tc
    nrows = MLA_HEADS * nq
    i, qb, c = pl.program_id(0), pl.program_id(1), pl.program_id(2)
    qpos = geom.q_base(qb) + _row_tok(nrows, nq)
    lane = _lane()

    def process():
        qa = jnp.concatenate([qa_ref[:, h * LANE:(h + 1) * LANE] for h in range(MLA_HEADS)], axis=0)
        qpe = []
        for h in range(MLA_HEADS):
            x = qpe_ref[:, (h // 4) * LANE:(h // 4 + 1) * LANE]
            if h % 4:
                x = pltpu.roll(x, LANE - (h % 4) * MLA_ROPE, 1)
            qpe.append(jnp.where(lane < MLA_ROPE, x, 0.0))
        qpe = jnp.concatenate(qpe, axis=0)
        ckv = _cat([_ld(r) for r in ckv_refs], 0)
        kpet = _cat([_ld(r) for r in kpe_refs], 1)
        cos, sin = cos_ref[...], sin_ref[...]
        kpos = c * tc + lax.broadcasted_iota(jnp.int32, (1, tc), 1)
        if geom.sample:
            cos = jnp.concatenate([cos, cosn_ref[0]], axis=1)
            sin = jnp.concatenate([sin, sinn_ref[0]], axis=1)
            kpos = jnp.concatenate([kpos, geom.new_kpos(i, c == geom.n_chunk - 1)], axis=1)
        t = ckv.shape[0]
        ckv_b = ckv.astype(BF16)
        knt = lax.dot_general(wukt_ref[...], ckv_b, (((1,), (1,)), ((), ())),
                              preferred_element_type=F32)
        ss = jnp.sum((knt * knt).reshape(MLA_HEADS, MLA_NOPE, t), axis=1)
        pe2 = jnp.sum(kpet * kpet, axis=0, keepdims=True)
        rt = lax.rsqrt((ss + pe2) * (1.0 / MLA_QK) + EPS)
        kg = kpet * gpe_ref[...]
        x1, x2 = kg[:MLA_ROPE // 2], kg[MLA_ROPE // 2:]
        kr = jnp.concatenate([x1 * cos - x2 * sin, x2 * cos + x1 * sin,
                              jnp.zeros((LANE - MLA_ROPE, t), F32)], axis=0)
        s = _dot_nt(qa, ckv_b) + _dot(qpe, kr)
        s = jnp.concatenate([s[h * nq:(h + 1) * nq] * rt[h:h + 1] for h in range(MLA_HEADS)], axis=0)
        s = _mask_tail(s, kpos, qpos, tc if geom.sample else 0)
        _flash_update(s, None, lambda p: _dot(p, ckv_b), m_sc, l_sc, acc_sc)

    @pl.when(c == 0)
    def _():
        _flash_init(m_sc, l_sc, acc_sc)

    if geom.sample:
        process()
    else:
        pl.when(geom.chunk_live(qb, c))(process)

    @pl.when(c == geom.n_chunk - 1)
    def _():
        lat = _flash_out(l_sc, acc_sc)
        for h in range(MLA_HEADS):
            o_ref[:, h * LANE:(h + 1) * LANE] = lat[h * nq:(h + 1) * nq]


def mla_attention(geom, li, pt, r, pools, tabs, lw):
    n, tc, tpr, nq = geom.n_ref, geom.tc, geom.tok_per_ref, geom.nq
    cos_t, sin_t, cos_n, sin_n = tabs
    in_specs = [pl.BlockSpec((nq, 8 * LANE), lambda i, q, c, pt: (geom.q_block(i, q), 0)),
                pl.BlockSpec((nq, 2 * LANE), lambda i, q, c, pt: (geom.q_block(i, q), 0))]
    args = [r["qa"], r["qpe"]]
    if geom.sample:
        in_specs += [POOL_SPEC, POOL_SPEC]
        args += [pools[0], pools[1]]
        noff = LANE // geom.seq
        in_specs += [pl.BlockSpec((LANE, LANE), lambda i, q, c, pt: (geom.new_block(i), 0)),
                     pl.BlockSpec((MLA_ROPE, LANE), lambda i, q, c, pt: (0, geom.new_block(i))),
                     pl.BlockSpec((1, MLA_ROPE // 2, LANE), lambda i, q, c, pt: (geom.new_off(i) // geom.seq, 0, 0)),
                     pl.BlockSpec((1, MLA_ROPE // 2, LANE), lambda i, q, c, pt: (geom.new_off(i) // geom.seq, 0, 0))]
        args += [r["ckv"], r["last_t"], cos_n, sin_n]
        assert cos_n.shape[0] == noff
        tab_spec = pl.BlockSpec((MLA_ROPE // 2, tc), lambda i, q, c, pt: (0, c))
    else:
        in_specs += [pl.BlockSpec((tc, LANE), lambda i, q, c, pt: (geom.chunk_block(i, q, c), 0)),
                     pl.BlockSpec((MLA_ROPE, tc), lambda i, q, c, pt: (0, geom.chunk_block(i, q, c)))]
        args += [r["ckv"], r["last_t"]]
        tab_spec = pl.BlockSpec((MLA_ROPE // 2, tc), lambda i, q, c, pt: (0, geom.chunk_block(0, q, c)))
    in_specs += [tab_spec, tab_spec,
                 pl.BlockSpec(lw["wukt"].shape, lambda i, q, c, pt: (0, 0)),
                 pl.BlockSpec((MLA_ROPE, 1), lambda i, q, c, pt: (0, 0))]
    args += [cos_t, sin_t, lw["wukt"], lw["k_gpe"]]
    nrows = MLA_HEADS * nq
    scratch = [pltpu.VMEM((nrows, 1), F32), pltpu.VMEM((nrows, 1), F32), pltpu.VMEM((nrows, LANE), F32)]
    if geom.sample:
        scratch += _page_scratch(geom, tpr) + _page_scratch(geom, MLA_ROPE)
    return pl.pallas_call(
        functools.partial(_mla_kernel, geom, li),
        grid_spec=pltpu.PrefetchScalarGridSpec(
            num_scalar_prefetch=1, grid=(geom.batch, geom.n_qb, geom.n_chunk), in_specs=in_specs,
            out_specs=pl.BlockSpec((nq, 8 * LANE), lambda i, q, c, pt: (geom.out_block(i, q), 0)),
            scratch_shapes=scratch),
        out_shape=jax.ShapeDtypeStruct((geom.batch * geom.seq, 8 * LANE), F32),
        compiler_params=_params(("arbitrary", "arbitrary", "arbitrary")),
        name="mla_sample" if geom.sample else "mla_prompt",
    )(pt, *args)


def _diff_kernel(geom, li, lam_init, pt_ref, q_ref, *refs):
    if geom.sample:
        pool, new_ref, lam_ref, o_ref, m_sc, l_sc, acc_sc, buf, sem = refs
        kv_refs = _paged_fetch(geom, li, pt_ref, pool, buf, sem, 0, 2) + [new_ref]
    else:
        kv_ref, lam_ref, o_ref, m_sc, l_sc, acc_sc = refs
        kv_refs = [kv_ref]
    nq, tc, tpr = geom.nq, geom.tc, geom.tok_per_ref
    nrows = 4 * nq
    i, qb, c = pl.program_id(0), pl.program_id(1), pl.program_id(2)
    qpos = geom.q_base(qb) + _row_tok(nrows, nq)
    lane = _lane()

    def rows_of(ref, k):
        rows = pl.ds(k, ref.shape[-2] // 4, stride=4)
        return ref[rows, :] if len(ref.shape) == 2 else ref[0, 0, rows, :]

    def process():
        kpos = c * tc + lax.broadcasted_iota(jnp.int32, (1, tc), 1)
        if geom.sample:
            kpos = jnp.concatenate([kpos, geom.new_kpos(i, c == geom.n_chunk - 1)], axis=1)
        for g in range(2):
            qg = []
            for comp in range(2):
                for rr in range(2):
                    x = q_ref[:, (2 * g + rr) * LANE:(2 * g + rr + 1) * LANE]
                    qg.append(jnp.where((lane < HEAD_DIM) if comp == 0 else (lane >= HEAD_DIM), x, 0.0))
            qg = jnp.concatenate(qg, axis=0)
            k_b = _cat([rows_of(r, g) for r in kv_refs], 0).astype(BF16)
            v_b = _cat([rows_of(r, 2 + g) for r in kv_refs], 0).astype(BF16)
            s = _mask_tail(_dot_nt(qg, k_b), kpos, qpos, tc if geom.sample else 0)
            _flash_update(s, None, lambda p: _dot(p, v_b), m_sc, l_sc, acc_sc, idx=g)

    @pl.when(c == 0)
    def _():
        _flash_init(m_sc, l_sc, acc_sc)

    if geom.sample:
        process()
    else:
        pl.when(geom.chunk_live(qb, c))(process)

    @pl.when(c == geom.n_chunk - 1)
    def _():
        lv = lam_ref[...]
        lam = (jnp.exp(jnp.sum(lv[0:1] * lv[1:2], axis=-1, keepdims=True))
               - jnp.exp(jnp.sum(lv[2:3] * lv[3:4], axis=-1, keepdims=True)) + lam_init)
        for g in range(2):
            o = _flash_out(l_sc, acc_sc, idx=g)
            o = o[:2 * nq] - lam * o[2 * nq:]
            for rr in range(2):
                o_ref[:, (2 * g + rr) * LANE:(2 * g + rr + 1) * LANE] = o[rr * nq:(rr + 1) * nq]


def diff_attention(geom, li, pt, r, diff_rows, pool, lam_vecs):
    n, tc, tpr, nq = geom.n_ref, geom.tc, geom.tok_per_ref, geom.nq
    in_specs = [pl.BlockSpec((nq, MIX_W), lambda i, q, c, pt: (geom.q_block(i, q), 0))]
    args = [r["q_diff"]]
    if geom.sample:
        in_specs.append(POOL_SPEC)
        args.append(pool)
        in_specs.append(pl.BlockSpec((4 * LANE, LANE), lambda i, q, c, pt: (geom.new_block(i), 0)))
    else:
        in_specs.append(pl.BlockSpec((4 * tc, LANE), lambda i, q, c, pt: (geom.chunk_block(i, q, c), 0)))
    args.append(diff_rows)
    in_specs.append(pl.BlockSpec((4, HEAD_DIM), lambda i, q, c, pt: (0, 0)))
    args.append(lam_vecs)
    scratch = [pltpu.VMEM((2, 4 * nq, 1), F32), pltpu.VMEM((2, 4 * nq, 1), F32),
               pltpu.VMEM((2, 4 * nq, LANE), F32)]
    if geom.sample:
        scratch += _page_scratch(geom, 4 * tpr)
    return pl.pallas_call(
        functools.partial(_diff_kernel, geom, li, _lambda_init(li)),
        grid_spec=pltpu.PrefetchScalarGridSpec(
            num_scalar_prefetch=1, grid=(geom.batch, geom.n_qb, geom.n_chunk), in_specs=in_specs,
            out_specs=pl.BlockSpec((nq, MIX_W), lambda i, q, c, pt: (geom.out_block(i, q), 0)),
            scratch_shapes=scratch),
        out_shape=jax.ShapeDtypeStruct((geom.batch * geom.seq, MIX_W), F32),
        compiler_params=_params(("arbitrary", "arbitrary", "arbitrary")),
        name="diff_sample" if geom.sample else "diff_prompt",
    )(pt, *args)


def _moba_kernel(geom, li, nb_cand, pt_ref, q_ref, *refs):
    if geom.sample:
        pool, new_ref, o_ref, m_sc, l_sc, km_sc, o_sc, buf, sem = refs
        kv_refs = _paged_fetch(geom, li, pt_ref, pool, buf, sem, 0, 2)
    else:
        kv_ref, o_ref, m_sc, l_sc, km_sc, o_sc = refs
        kv_refs = [kv_ref]
    nq, tc = geom.nq, geom.tc
    nrows = N_HEADS * nq
    nbc = tc // MOBA_BLOCK
    i, qb, c = pl.program_id(0), pl.program_id(1), pl.program_id(2)
    q_base = geom.q_base(qb)
    qpos = q_base + _row_tok(nrows, nq)
    cur = q_base // MOBA_BLOCK
    lane = _lane()
    q = _rows_from_heads(q_ref[...], MOBA_PLACE)

    def block(kvt, blk, kpos):
        kt, vt = kvt[:LANE], kvt[LANE:]
        s = _dot(q, kt)
        if kpos is not None:
            s = jnp.where(kpos <= qpos, s, NEG)
        mb = jnp.max(s, axis=-1, keepdims=True)
        p = jnp.exp(s - mb)
        here = lane == blk
        m_sc[...] = jnp.where(here, mb, m_sc[...])
        l_sc[...] = jnp.where(here, jnp.sum(p, axis=-1, keepdims=True), l_sc[...])
        km_sc[...] = jnp.where(here, jnp.sum(kt, axis=-1, keepdims=True) * (1.0 / MOBA_BLOCK), km_sc[...])
        o_sc[blk] = _dot_nt(p, vt)

    @pl.when((i == 0) & (qb == 0) & (c == 0))
    def _():
        m_sc[...] = jnp.full(m_sc.shape, NEG, F32)
        l_sc[...] = jnp.zeros(l_sc.shape, F32)
        km_sc[...] = jnp.zeros(km_sc.shape, F32)
        if not geom.sample:
            o_sc[...] = jnp.zeros(o_sc.shape, F32)

    if geom.sample:
        block(new_ref[...], nb_cand, geom.new_kpos(i, True))
        ppb = MOBA_BLOCK // geom.tok_per_ref
        for j in range(nbc):
            block(_cat([_ld(r) for r in kv_refs[j * ppb:(j + 1) * ppb]], 1), c * nbc + j, None)
    else:
        for j in range(nbc):
            blk = c * nbc + j

            @pl.when(blk <= cur)
            def _():
                kvt = kv_refs[0][:, j * MOBA_BLOCK:(j + 1) * MOBA_BLOCK]
                kpos = blk * MOBA_BLOCK + lax.broadcasted_iota(jnp.int32, (1, MOBA_BLOCK), 1)
                block(kvt, blk, kpos)

    @pl.when(c == geom.n_chunk - 1)
    def _():
        gate = _dot_xx(q, km_sc[...])
        past = lane < cur
        rank = jnp.zeros(gate.shape, jnp.int32)
        for mth in range(nb_cand):
            gm = gate[:, mth:mth + 1]
            ahead = (gm > gate) | ((gm == gate) & (mth < lane))
            if not geom.sample:
                ahead = ahead & (mth < cur)
            rank = rank + ahead.astype(jnp.int32)
        sel = (past & (rank < MOBA_TOPK) & (jnp.abs(gate) < jnp.inf)) | (lane == cur)
        mm_ = jnp.where(sel, m_sc[...], NEG)
        mtop = jnp.max(mm_, axis=-1, keepdims=True)
        w = jnp.where(sel, jnp.exp(mm_ - mtop), 0.0)
        den = jnp.sum(w * l_sc[...], axis=-1, keepdims=True)
        acc = jnp.zeros((nrows, LANE), F32)
        for blk in range(nb_cand + 1 if geom.sample else nb_cand):
            acc = acc + w[:, blk:blk + 1] * o_sc[blk]
        o_ref[...] = _heads_from_rows(acc / jnp.maximum(den, 1e-30), MOBA_PLACE, nq)


def moba_attention(geom, li, pt, r, pool):
    n, tc, tpr, nq = geom.n_ref, geom.tc, geom.tok_per_ref, geom.nq
    nrows = N_HEADS * nq
    nb_cand = geom.n_chunk * (tc // MOBA_BLOCK)
    assert nb_cand + 1 <= LANE
    in_specs = [pl.BlockSpec((nq, MIX_W), lambda i, q, c, pt: (geom.q_block(i, q), 0))]
    args = [r["q_moba"]]
    if geom.sample:
        in_specs.append(POOL_SPEC)
        args.append(pool)
        in_specs.append(pl.BlockSpec((2 * LANE, LANE), lambda i, q, c, pt: (0, geom.new_block(i))))
    else:
        in_specs.append(pl.BlockSpec((2 * LANE, tc), lambda i, q, c, pt: (0, geom.chunk_block(i, q, c))))
    args.append(r["moba_kv_t"])
    scratch = [pltpu.VMEM((nrows, LANE), F32), pltpu.VMEM((nrows, LANE), F32),
               pltpu.VMEM((LANE, LANE), F32), pltpu.VMEM((nb_cand + 1, nrows, LANE), F32)]
    if geom.sample:
        scratch += _page_scratch(geom, 2 * LANE)
    return pl.pallas_call(
        functools.partial(_moba_kernel, geom, li, nb_cand),
        grid_spec=pltpu.PrefetchScalarGridSpec(
            num_scalar_prefetch=1, grid=(geom.batch, geom.n_qb, geom.n_chunk), in_specs=in_specs,
            out_specs=pl.BlockSpec((nq, MIX_W), lambda i, q, c, pt: (geom.out_block(i, q), 0)),
            scratch_shapes=scratch),
        out_shape=jax.ShapeDtypeStruct((geom.batch * geom.seq, MIX_W), F32),
        compiler_params=_params(("arbitrary", "arbitrary", "arbitrary")),
        name="moba_sample" if geom.sample else "moba_prompt",
    )(pt, *args)


def _gelu_tanh(x):
    return 0.5 * x * (1.0 + jnp.tanh(math.sqrt(2.0 / math.pi) * (x + 0.044715 * x * x * x)))


def _compress_kernel(geom, li, nseg, pt_ref, src_ref, *refs):
    (ptop_ref, pbot_ref, wtop_ref, wbot_ref, b1_ref, w2_ref, kg_ref, c_ref, sa_ref, sb_ref,
     o_ref, x_sc, a_sc, b_sc) = refs[:14]
    if geom.sample:
        buf, sem = refs[14:]
        kv_refs = _paged_fetch(geom, li, pt_ref, src_ref, buf, sem, 0, 1)
    else:
        kv_refs = [src_ref]
    tc, tpr = geom.tc, geom.tok_per_ref
    segc = tc // NSA_CMP_STRIDE
    c = pl.program_id(1)
    for j, r in enumerate(kv_refs):
        x_sc[j * tpr:(j + 1) * tpr, :] = _ld(r).T if geom.sample else _ld(r)
    u = jnp.concatenate([x_sc[pl.ds(r, segc, stride=NSA_CMP_STRIDE), :] for r in range(NSA_CMP_STRIDE)], axis=1)
    row0 = pl.multiple_of(c * segc, 8)
    a_sc[pl.ds(row0, segc), :] = _dot(u + ptop_ref[...], wtop_ref[...])
    b_sc[pl.ds(row0, segc), :] = _dot(u + pbot_ref[...], wbot_ref[...])

    @pl.when(c == geom.n_chunk - 1)
    def _():
        hid = a_sc[...] + pltpu.roll(b_sc[...], nseg - 1, 0) + b1_ref[...]
        kv = _dot(_gelu_tanh(hid), w2_ref[...])
        lane = _lane()
        ri = lax.broadcasted_iota(jnp.int32, (LANE, LANE), 0) // HEAD_DIM
        ci = lax.broadcasted_iota(jnp.int32, (LANE, LANE), 1) // HEAD_DIM
        ss = _dot_x(kv * kv, (ri == ci).astype(BF16))
        y = kv * lax.rsqrt(ss * (1.0 / HEAD_DIM) + EPS) * kg_ref[...]
        y = _rope_apply(y, c_ref[...], sa_ref[...], sb_ref[...], ROT_DIM // 2)
        o_ref[0] = jnp.where(lane < HEAD_DIM, y, kv)


def nsa_compress(geom, li, pt, r, pool, nseg, lw, tabs):
    n, tc, tpr = geom.n_ref, geom.tc, geom.tok_per_ref
    if geom.sample:
        in_specs = [POOL_SPEC]
        args = [pool]
    else:
        in_specs = [pl.BlockSpec((tc, LANE), lambda i, c, pt: ((geom.row0 + i * geom.seq) // tc + c, 0))]
        args = [r["nsa_kv"]]
    consts = (lw["cmp_ptop"], lw["cmp_pbot"], lw["cmp_wtop"], lw["cmp_wbot"], lw["cmp_b1"], lw["cmp_w2"],
              lw["cmp_kg"]) + tuple(tabs)
    in_specs += [pl.BlockSpec(a.shape, lambda i, c, pt: (0, 0)) for a in consts]
    scratch = [pltpu.VMEM((tc, LANE), F32), pltpu.VMEM((nseg, 2 * LANE), F32), pltpu.VMEM((nseg, 2 * LANE), F32)]
    if geom.sample:
        scratch += _page_scratch(geom, LANE)
    return pl.pallas_call(
        functools.partial(_compress_kernel, geom, li, nseg),
        grid_spec=pltpu.PrefetchScalarGridSpec(
            num_scalar_prefetch=1, grid=(geom.batch, geom.n_chunk), in_specs=in_specs,
            out_specs=pl.BlockSpec((1, nseg, LANE), lambda i, c, pt: (i, 0, 0)),
            scratch_shapes=scratch),
        out_shape=jax.ShapeDtypeStruct((geom.batch, nseg, LANE), F32),
        compiler_params=_params(("arbitrary", "arbitrary")),
        name="nsa_compress_sample" if geom.sample else "nsa_compress_prompt",
    )(pt, *args, *consts)


def _win_kernel(geom, pt_ref, q_ref, *refs):
    o_ref = refs[-1]
    kv_refs = refs[:-1]
    nq = geom.nq
    nrows = N_HEADS * nq
    i, qb = pl.program_id(0), pl.program_id(1)
    q_base = geom.q_base(qb)
    qpos = q_base + _row_tok(nrows, nq)
    kvt = jnp.concatenate([_ld(r) for r in kv_refs], axis=1)
    if geom.sample:
        kpos = jnp.concatenate([geom.past_len - NSA_WINDOW + lax.broadcasted_iota(jnp.int32, (1, NSA_WINDOW), 1),
                                geom.new_kpos(i, True)], axis=1)
    else:
        kpos = q_base - NSA_WINDOW + lax.broadcasted_iota(jnp.int32, (1, kvt.shape[1]), 1)
    d = qpos - kpos
    mask = (d >= 0) & (d < NSA_WINDOW) & (kpos >= 0)
    s = jnp.where(mask, _dot(_rows_from_heads(q_ref[...], NSA_PLACE), kvt), NEG)
    p = jnp.where(mask, jnp.exp(s - jnp.max(s, axis=-1, keepdims=True)), 0.0)
    o = _dot_nt(p, kvt)
    o_ref[0, 0] = o / jnp.maximum(jnp.sum(p, axis=-1, keepdims=True), 1e-30)


def nsa_window(geom, li, pt, r, win_state):
    nq = geom.nq
    nrows = N_HEADS * nq
    in_specs = [pl.BlockSpec((nq, MIX_W), lambda i, q, pt: (geom.q_block(i, q), 0))]
    if geom.sample:
        in_specs += [pl.BlockSpec((1, 1, LANE, NSA_WINDOW), lambda i, q, pt: (li, i, 0, 0)),
                     pl.BlockSpec((LANE, LANE), lambda i, q, pt: (0, geom.new_block(i)))]
        args = [win_state, r["nsa_win_t"]]
    else:
        nback = NSA_WINDOW // nq
        in_specs += [pl.BlockSpec((LANE, nq),
                                  functools.partial(lambda i, q, pt, j: (0, geom.q_block(i, 0) + jnp.maximum(q - nback + j, 0)),
                                                    j=j))
                     for j in range(nback + 1)]
        args = [r["nsa_win_t"]] * (nback + 1)
    return pl.pallas_call(
        functools.partial(_win_kernel, geom),
        grid_spec=pltpu.PrefetchScalarGridSpec(
            num_scalar_prefetch=1, grid=(geom.batch, geom.n_qb), in_specs=in_specs,
            out_specs=pl.BlockSpec((1, 1, nrows, LANE), lambda i, q, pt: (i, q, 0, 0))),
        out_shape=jax.ShapeDtypeStruct((geom.batch, geom.n_qb, nrows, LANE), F32),
        compiler_params=_params(("arbitrary", "arbitrary")),
        name="nsa_window_sample" if geom.sample else "nsa_window_prompt",
    )(pt, r["q_nsa"], *args)


def _nsa_kernel(geom, li, nc, nsb, pt_ref, q_ref, kvc_ref, gate_ref, owin_ref, *refs):
    if geom.sample:
        pool, new_ref, o_ref, m_sc, l_sc, acc_sc, ocmp_sc, sel_sc, buf, sem = refs
        kv_refs = _paged_fetch(geom, li, pt_ref, pool, buf, sem, LANE, 2) + [new_ref]
    else:
        kv_ref, o_ref, m_sc, l_sc, acc_sc, ocmp_sc, sel_sc = refs
        kv_refs = [kv_ref]
    nq, tc = geom.nq, geom.tc
    nrows = N_HEADS * nq
    nsbp = sel_sc.shape[1]
    i, qb, c = pl.program_id(0), pl.program_id(1), pl.program_id(2)
    q_base = geom.q_base(qb)
    qtok = q_base + lax.broadcasted_iota(jnp.int32, (nq, 1), 0)
    q = _rows_from_heads(q_ref[...], NSA_PLACE)

    def select_and_compress():
        kvc = kvc_ref[0]
        ncp = kvc.shape[0]
        ci = lax.broadcasted_iota(jnp.int32, (1, ncp), 1)
        cend = ci * NSA_CMP_STRIDE + (NSA_CMP_LEN - 1)
        mask = ((cend <= qtok) & (ci < nc))[None]
        s = _dot_nt(q, kvc).reshape(N_HEADS, nq, ncp)
        s = jnp.where(mask, s, NEG)
        e = jnp.where(mask, jnp.exp(s - jnp.max(s, axis=-1, keepdims=True)), 0.0)
        p = e / jnp.maximum(jnp.sum(e, axis=-1, keepdims=True), 1e-30)
        ocmp_sc[...] = _dot(p.reshape(nrows, ncp), kvc)
        cs = lax.broadcasted_iota(jnp.int32, (ncp, 1), 0) * NSA_CMP_STRIDE
        bs = lax.broadcasted_iota(jnp.int32, (1, nsbp), 1) * NSA_SEL_BLOCK
        covers = ((cs < bs + NSA_SEL_BLOCK) & (cs + NSA_CMP_LEN > bs)).astype(BF16)
        imp = _dot_x(jnp.sum(p, axis=0), covers)
        blk = lax.broadcasted_iota(jnp.int32, (1, nsbp), 1)
        curb = qtok // NSA_SEL_BLOCK
        forced = (blk == curb) | (blk == 0)
        score = jnp.where(forced, SEL_FORCE, jnp.where(blk <= curb, imp, -jnp.inf))
        rank = jnp.zeros(score.shape, jnp.int32)
        for mth in range(nsb):
            sm = score[:, mth:mth + 1]
            rank = rank + ((sm > score) | ((sm == score) & (mth < blk))).astype(jnp.int32)
        sel_sc[...] = ((rank < NSA_TOPN) & (score > -jnp.inf)).astype(F32)

    def process():
        kvt = _cat([_ld(r) for r in kv_refs], 1)
        kpos = c * tc + lax.broadcasted_iota(jnp.int32, (1, tc), 1)
        if geom.sample:
            kpos = jnp.concatenate([kpos, geom.new_kpos(i, c == geom.n_chunk - 1)], axis=1)
        if geom.sample and geom.n_chunk == 1:
            sel = sel_sc[...]
            low = _lane() < NSA_SEL_BLOCK
            cols = [jnp.where(low, sel[:, 2 * j:2 * j + 1], sel[:, 2 * j + 1:2 * j + 2]) for j in range(tc // LANE)]
            nb = geom.past_len // NSA_SEL_BLOCK
            cols.append(jnp.broadcast_to(sel[:, nb:nb + 1], (nq, LANE)))
            picked = jnp.concatenate(cols, axis=1) > 0.5
        else:
            expand = (lax.broadcasted_iota(jnp.int32, (nsbp, 1), 0) == kpos // NSA_SEL_BLOCK).astype(BF16)
            picked = jnp.dot(sel_sc[...].astype(BF16), expand, preferred_element_type=F32) > 0.5
        mask = picked & (kpos <= qtok)
        _flash_update(_dot(q, kvt), mask, lambda p: _dot_nt(p, kvt), m_sc, l_sc, acc_sc)

    @pl.when(c == 0)
    def _():
        _flash_init(m_sc, l_sc, acc_sc)
        select_and_compress()

    if geom.sample:
        process()
    else:
        pl.when(geom.chunk_live(qb, c))(process)

    @pl.when(c == geom.n_chunk - 1)
    def _():
        g = gate_ref[...]
        osel = _flash_out(l_sc, acc_sc)
        outs = []
        for h in range(N_HEADS):
            sl = slice(h * nq, (h + 1) * nq)
            g0 = GATE_LANE0 + 3 * h
            outs.append(g[:, g0:g0 + 1] * ocmp_sc[sl] + g[:, g0 + 1:g0 + 2] * osel[sl]
                        + g[:, g0 + 2:g0 + 3] * owin_ref[0, 0, sl])
        o_ref[...] = _heads_from_rows(jnp.concatenate(outs, axis=0), NSA_SRC, nq)


def nsa_attention(geom, li, pt, r, kvc, o_win, pool, nc, nsb):
    n, tc, tpr, nq = geom.n_ref, geom.tc, geom.tok_per_ref, geom.nq
    nrows = N_HEADS * nq
    nsbp = -(-nsb // LANE) * LANE
    in_specs = [pl.BlockSpec((nq, MIX_W), lambda i, q, c, pt: (geom.q_block(i, q), 0)),
                pl.BlockSpec((1,) + kvc.shape[1:], lambda i, q, c, pt: (i, 0, 0)),
                pl.BlockSpec((nq, LANE), lambda i, q, c, pt: (geom.q_block(i, q), 0)),
                pl.BlockSpec((1, 1, nrows, LANE), lambda i, q, c, pt: (i, q, 0, 0))]
    args = [r["q_nsa"], kvc, r["last"], o_win]
    if geom.sample:
        in_specs.append(POOL_SPEC)
        args.append(pool)
        in_specs.append(pl.BlockSpec((LANE, LANE), lambda i, q, c, pt: (1, geom.new_block(i))))
    else:
        in_specs.append(pl.BlockSpec((LANE, tc), lambda i, q, c, pt: (1, geom.chunk_block(i, q, c))))
    args.append(r["nsa_kv_t"])
    scratch = [pltpu.VMEM((nrows, 1), F32), pltpu.VMEM((nrows, 1), F32), pltpu.VMEM((nrows, LANE), F32),
               pltpu.VMEM((nrows, LANE), F32), pltpu.VMEM((nq, nsbp), F32)]
    if geom.sample:
        scratch += _page_scratch(geom, LANE)
    return pl.pallas_call(
        functools.partial(_nsa_kernel, geom, li, nc, nsb),
        grid_spec=pltpu.PrefetchScalarGridSpec(
            num_scalar_prefetch=1, grid=(geom.batch, geom.n_qb, geom.n_chunk), in_specs=in_specs,
            out_specs=pl.BlockSpec((nq, MIX_W), lambda i, q, c, pt: (geom.out_block(i, q), 0)),
            scratch_shapes=scratch),
        out_shape=jax.ShapeDtypeStruct((geom.batch * geom.seq, MIX_W), F32),
        compiler_params=_params(("arbitrary", "arbitrary", "arbitrary")),
        name="nsa_sample" if geom.sample else "nsa_prompt",
    )(pt, *args)


def _rope_tables(pos, half, theta, period, n_rep):
    inv = theta ** (-jnp.arange(half, dtype=F32) / half)
    ang = pos.astype(F32)[:, None] * inv[None, :]
    cos, sin = jnp.cos(ang), jnp.sin(ang)
    t = pos.shape[0]
    ones = jnp.ones((t, period - 2 * half), F32)
    zeros = jnp.zeros((t, period - 2 * half), F32)
    zh = jnp.zeros((t, half), F32)
    c = jnp.concatenate([cos, cos, ones], axis=1)
    sa = jnp.concatenate([-sin, zh, zeros], axis=1)
    sb = jnp.concatenate([zh, sin, zeros], axis=1)
    return tuple(jnp.tile(a, (1, n_rep)) for a in (c, sa, sb))


def _layer_weights(li, p):
    w_in = p["w_in"][li]
    d = w_in.shape[0]
    wa = jnp.concatenate([w_in[:, 0:512], w_in[:, 544:3232], w_in[:, 512:544], w_in[:, 3232:3256],
                          jnp.zeros((d, LANE - 56), F32)], axis=1).astype(BF16)
    lw = {"wa": wa, "wg": w_in[:, 3256:].astype(BF16)}
    lw["cq_g"] = p["mla_cq_g"][li].reshape(1, -1)
    lw["ckv_g"] = p["mla_ckv_g"][li].reshape(1, -1)
    wuq = p["mla_w_uq"][li]
    lw["wuq"] = jnp.concatenate([wuq[:, :, :MLA_ROPE].reshape(384, -1),
                                 wuq[:, :, MLA_ROPE:].reshape(384, -1)], axis=1).astype(BF16)
    head = jnp.concatenate([jnp.arange(256) // MLA_ROPE, jnp.arange(512) // MLA_NOPE])
    lw["b768"] = (head[:, None] == head[None, :]).astype(BF16)
    qg = p["mla_qn_g"][li]
    lw["q_g768"] = jnp.concatenate([jnp.tile(qg[:MLA_ROPE], 8), jnp.tile(qg[MLA_ROPE:], 8)]).reshape(1, -1)
    kg = p["mla_kn_g"][li]
    wuk = p["mla_w_uk"][li]
    hh = jnp.arange(MLA_HEADS)
    wqa = jnp.zeros((8, MLA_NOPE, 8, LANE), F32)
    wqa = wqa.at[hh, :, hh, :].set(jnp.transpose(wuk, (1, 2, 0)) * kg[MLA_ROPE:][None, :, None])
    lw["wqa"] = wqa.reshape(8 * MLA_NOPE, 8 * LANE).astype(BF16)
    lw["wukt"] = jnp.transpose(wuk, (1, 2, 0)).reshape(8 * MLA_NOPE, LANE).astype(BF16)
    lw["k_gpe"] = kg[:MLA_ROPE].reshape(MLA_ROPE, 1)
    wuv = jnp.zeros((8, LANE, 8, HEAD_DIM), F32)
    wuv = wuv.at[hh, :, hh, :].set(jnp.transpose(p["mla_w_uv"][li], (1, 0, 2)))
    lw["wuv"] = wuv.reshape(8 * LANE, MIX_W).astype(BF16)
    gains = {"q_moba": p["moba_qn_g"][li], "moba_kv": p["moba_kn_g"][li], "q_diff": p["diff_qn_g"][li],
             "diff_kv": p["diff_kn_g"][li], "q_nsa": p["nsa_qn_g"][li]}
    one = jnp.ones((HEAD_DIM,), F32)
    sg = []
    for name, width, en in SLAB_GROUPS:
        for k, e in enumerate(en):
            if name == "nsa_kv":
                sg.append(p["nsa_kn_g"][li, 1] if e else one)
            elif name == "nsa_win":
                sg.append(p["nsa_kn_g"][li, 2] if e else one)
            else:
                sg.append(gains[name] if e else one)
    lw["slab_g"] = jnp.concatenate(sg).reshape(1, -1)
    w1 = p["nsa_cmp_w1"][li].reshape(2, NSA_CMP_LEN, HEAD_DIM, -1)
    hid = w1.shape[-1]
    wkv = jnp.zeros((NSA_CMP_LEN, 2, HEAD_DIM, 2, hid), F32)
    wkv = wkv.at[:, 0, :, 0, :].set(w1[0]).at[:, 1, :, 1, :].set(w1[1])
    wkv = wkv.reshape(NSA_CMP_LEN * 2 * HEAD_DIM, 2 * hid)
    half = NSA_CMP_STRIDE * 2 * HEAD_DIM
    lw["cmp_wtop"] = wkv[:half].astype(BF16)
    lw["cmp_wbot"] = wkv[half:].astype(BF16)
    pos = jnp.transpose(p["nsa_cmp_pos"][li], (1, 0, 2)).reshape(1, NSA_CMP_LEN * 2 * HEAD_DIM)
    lw["cmp_ptop"] = pos[:, :half]
    lw["cmp_pbot"] = pos[:, half:]
    lw["cmp_b1"] = p["nsa_cmp_b1"][li].reshape(1, 2 * hid)
    w2 = p["nsa_cmp_w2"][li]
    w2b = jnp.zeros((2, hid, 2, HEAD_DIM), F32).at[0, :, 0, :].set(w2[0]).at[1, :, 1, :].set(w2[1])
    lw["cmp_w2"] = w2b.reshape(2 * hid, 2 * HEAD_DIM).astype(BF16)
    lw["cmp_kg"] = jnp.concatenate([p["nsa_kn_g"][li, 0], one]).reshape(1, LANE)
    lw["w_branch"] = p["w_branch"][li].astype(BF16)
    lw["w_out"] = p["w_out"][li].astype(BF16)
    lw["w_gu"] = p["ffn_w_gu"][li].astype(BF16)
    lw["w_down"] = p["ffn_w_down"][li].astype(BF16)
    return lw


def _attention_group(geoms, li, pt, r, diff_rows, pools, nseg, nsb, tabs, lw, lam_vecs):
    lat = mla_attention(geoms["mla"], li, pt, r, pools.get("mla"), tabs["mla"], lw)
    o_mla = mm(lat, lw["wuv"])
    o_moba = moba_attention(geoms["moba"], li, pt, r, pools.get("moba"))
    o_diff = diff_attention(geoms["diff"], li, pt, r, diff_rows, pools.get("diff"), lam_vecs)
    o_win = nsa_window(geoms["nsa"], li, pt, r, pools.get("win"))
    kvc = nsa_compress(geoms["cmp"], li, pt, r, pools.get("nsa"), nseg, lw, tabs["cmp"])
    o_nsa = nsa_attention(geoms["nsa"], li, pt, r, kvc, o_win, pools.get("nsa"), nseg - 1, nsb)
    return o_mla, o_moba, o_diff, o_nsa


def kernel(x_prompt, x_sample, cache_mla_ckv, cache_mla_kpe, cache_moba_kv, cache_diff_kv, cache_nsa_kv, state_nsa_win, page_table, ln_attn_g, w_in, mla_cq_g, mla_ckv_g, mla_w_uq, mla_qn_g, mla_w_uk, mla_w_uv, mla_kn_g, moba_qn_g, moba_kn_g, diff_qn_g, diff_kn_g, diff_lambda, diff_subln_g, nsa_qn_g, nsa_kn_g, nsa_cmp_pos, nsa_cmp_w1, nsa_cmp_b1, nsa_cmp_w2, w_branch, w_out, ln_ffn_g, ffn_w_gu, ffn_w_down):
    p = dict(w_in=w_in, mla_cq_g=mla_cq_g, mla_ckv_g=mla_ckv_g, mla_w_uq=mla_w_uq, mla_qn_g=mla_qn_g,
             mla_w_uk=mla_w_uk, mla_w_uv=mla_w_uv, mla_kn_g=mla_kn_g, moba_qn_g=moba_qn_g,
             moba_kn_g=moba_kn_g, diff_qn_g=diff_qn_g, diff_kn_g=diff_kn_g, nsa_qn_g=nsa_qn_g,
             nsa_kn_g=nsa_kn_g, nsa_cmp_pos=nsa_cmp_pos, nsa_cmp_w1=nsa_cmp_w1, nsa_cmp_b1=nsa_cmp_b1,
             nsa_cmp_w2=nsa_cmp_w2, w_branch=w_branch, w_out=w_out, ffn_w_gu=ffn_w_gu, ffn_w_down=ffn_w_down)
    bp, tp, d = x_prompt.shape
    bs, ts, _ = x_sample.shape
    depth, n_pool, page = cache_mla_ckv.shape[:3]
    n_pages = page_table.shape[1]
    past = n_pages * page
    mp, ms = bp * tp, bs * ts
    assert page == LANE and past % MOBA_BLOCK == 0 and state_nsa_win.shape[2] == NSA_WINDOW
    assert tp % MOBA_BLOCK == 0 and tp >= NSA_WINDOW and LANE % ts == 0 and ts % 8 == 0
    assert mp % LANE == 0 and ms % LANE == 0 and ts <= NSA_SEL_BLOCK
    nq_p = min(Q_BLOCK, tp)
    n_ref = min(PAGES_PER_STEP, n_pages)
    assert n_pages % n_ref == 0 and (n_ref * page) % MOBA_BLOCK == 0
    tc_p = min(512, tp)
    geom_p = _Geom(False, bp, tp, 0, nq_p, tc_p, 1, tp // tc_p, 0)
    geom_pw = _Geom(False, bp, tp, 0, nq_p, tp, 1, 1, 0)
    geom_s = _Geom(True, bs, ts, mp, ts, page, n_ref, n_pages // n_ref, past)
    geom_sw = _Geom(True, bs, ts, mp, ts, page, n_pages, 1, past)
    geoms_p = {"mla": geom_p, "diff": geom_p, "nsa": geom_p, "moba": geom_pw, "cmp": geom_pw}
    geoms_s = {"mla": geom_s, "diff": geom_s, "nsa": geom_sw, "moba": geom_sw, "cmp": geom_sw}
    pt_dummy = jnp.zeros((1, 1), jnp.int32)

    fm = lambda a: jnp.transpose(a, (0, 1, 3, 4, 5, 2)).reshape(a.shape[0], a.shape[1], -1, a.shape[2])
    pools_s = {"moba": fm(cache_moba_kv),
               "nsa": fm(cache_nsa_kv),
               "win": fm(state_nsa_win),
               "mla": (cache_mla_ckv, jnp.transpose(cache_mla_kpe, (0, 1, 3, 2))),
               "diff": cache_diff_kv.reshape(depth, n_pool, 4 * page, LANE)}

    pos_all = jnp.concatenate([jnp.tile(jnp.arange(tp), bp), jnp.tile(past + jnp.arange(ts), bs)])
    row_tabs = (_rope_tables(pos_all, ROT_DIM // 2, ROPE_THETA, HEAD_DIM, 2)
                + _rope_tables(pos_all, MLA_ROPE // 2, MLA_THETA, MLA_ROPE, 4))
    inv_m = (MLA_THETA ** (-jnp.arange(MLA_ROPE // 2, dtype=F32) / (MLA_ROPE // 2)))[:, None]
    ang_p = inv_m * jnp.arange(max(tp, past), dtype=F32)[None, :]
    cos_k, sin_k = jnp.cos(ang_p), jnp.sin(ang_p)
    rel = jnp.arange(LANE)[None, :] - (jnp.arange(LANE // ts) * ts)[:, None]
    ang_n = inv_m[None] * (past + rel).astype(F32)[:, None, :]
    mla_tabs_p = (cos_k[:, :tp], sin_k[:, :tp], None, None)
    mla_tabs_s = (cos_k[:, :past], sin_k[:, :past], jnp.cos(ang_n), jnp.sin(ang_n))

    def cmp_tabs(nseg):
        cend = jnp.arange(nseg) * NSA_CMP_STRIDE + NSA_CMP_LEN - 1
        c, sa, sb = _rope_tables(cend, ROT_DIM // 2, ROPE_THETA, HEAD_DIM, 1)
        one = jnp.ones((nseg, HEAD_DIM), F32)
        zero = jnp.zeros((nseg, HEAD_DIM), F32)
        return (jnp.concatenate([c, one], 1), jnp.concatenate([sa, zero], 1), jnp.concatenate([sb, zero], 1))

    nseg_p, nseg_s = tp // NSA_CMP_STRIDE, past // NSA_CMP_STRIDE
    tabs_p = {"mla": mla_tabs_p, "cmp": cmp_tabs(nseg_p)}
    tabs_s = {"mla": mla_tabs_s, "cmp": cmp_tabs(nseg_s)}

    x = jnp.concatenate([x_prompt.reshape(mp, d), x_sample.reshape(ms, d)], axis=0)
    news = {k: [] for k in ("ckv", "kpe", "moba", "diff", "nsa", "win")}
    for li in range(depth):
        lw = _layer_weights(li, p)
        z = mm_norm(x, ln_attn_g[li], lw["wa"])
        r = token_rows(z, lw, row_tabs)
        for k, src in (("ckv", "ckv"), ("moba", "moba_kv"), ("diff", "diff_kv"), ("nsa", "nsa_kv"),
                       ("win", "nsa_win")):
            news[k].append(r[src])
        news["kpe"].append(r["last"][:, :MLA_ROPE])
        diff_rows = r["diff_kv"].reshape(4 * (mp + ms), LANE)
        op = _attention_group(geoms_p, li, pt_dummy, r, diff_rows, {}, nseg_p,
                              -(-tp // NSA_SEL_BLOCK), tabs_p, lw, diff_lambda[li])
        osm = _attention_group(geoms_s, li, page_table, r, diff_rows, pools_s, nseg_s,
                               -(-(past + ts) // NSA_SEL_BLOCK), tabs_s, lw, diff_lambda[li])
        y = merge_branches(x, ln_attn_g[li], op, osm, diff_subln_g[li], 1.0 - _lambda_init(li), lw["wg"],
                           lw["w_branch"])
        x = mm_res(y, lw["w_out"], x)
        act = mm_norm_swiglu(x, ln_ffn_g[li], lw["w_gu"])
        x = mm_res(act, lw["w_down"], x)

    def stack(k, tail):
        ps = jnp.stack([a[:mp].reshape((bp, tp) + tail) for a in news[k]])
        ss = jnp.stack([a[mp:].reshape((bs, ts) + tail) for a in news[k]])
        return ps, ss

    ckv_p, ckv_s = stack("ckv", (LANE,))
    kpe_p, kpe_s = stack("kpe", (MLA_ROPE,))
    moba_p, moba_s = stack("moba", (2, 2, HEAD_DIM))
    diff_p, diff_s = stack("diff", (2, 2, 2 * HEAD_DIM))
    nsa_p, nsa_s = stack("nsa", (4, 1, HEAD_DIM))
    win_p, win_s = stack("win", (2, 1, HEAD_DIM))
    win_p = win_p[:, :, -NSA_WINDOW:]
    win_s = jnp.concatenate([state_nsa_win, win_s], axis=2)[:, :, -NSA_WINDOW:]
    return (x[:mp].reshape(bp, tp, d), x[mp:].reshape(bs, ts, d),
            ckv_p, ckv_s, kpe_p, kpe_s, moba_p, moba_s, diff_p, diff_s, nsa_p, nsa_s, win_p, win_s)
```
